```python
import math
import jax, jax.numpy as jnp
from jax import lax
import numpy as np

D_MODEL = 2048
BATCH = 2
SEQ = 8192
DEPTH = 2

N_MIXERS = 2
N_MLSTM = (DEPTH + 1) // 2
N_GDN = DEPTH // 2
CHUNK = 64
EPS = 1e-6
ML_H = 4
ML_DV = D_MODEL // ML_H
ML_DQK = ML_DV // 2
GATE_SOFTCAP = 15.0
GK_H = 16
GV_H = 32
GD_K = 128
GD_V = 128
CONV_K = 4
MEM_LEN = 256
XA_H = 4
XA_DH = 128
D_FF = 4 * D_MODEL

ML_SPLITS = np.cumsum([ML_H * ML_DQK, ML_H * ML_DQK, ML_H * ML_DV, ML_H * ML_DV, ML_H]).tolist()
ML_IN = 2 * ML_H * ML_DQK + 2 * ML_H * ML_DV + 2 * ML_H
GDN_QKV = 2 * GK_H * GD_K + GV_H * GD_V
GDN_SPLITS = np.cumsum([GDN_QKV, GV_H * GD_V, GV_H]).tolist()
GDN_IN = GDN_QKV + GV_H * GD_V + 2 * GV_H

kernel_name = "hybrid_mlstm_gdn_memxattn_sandwich"


def rms_norm(x, g):
    xf = x.astype(jnp.float32)
    y = xf * lax.rsqrt(jnp.mean(xf * xf, axis=-1, keepdims=True) + EPS)
    return (y * g.astype(jnp.float32)).astype(x.dtype)


def l2_norm(x):
    return x * lax.rsqrt(jnp.sum(x * x, axis=-1, keepdims=True) + EPS)


def to_chunks(t):
    b, s, h = t.shape[:3]
    t = t.reshape((b, s // CHUNK, CHUNK, h) + t.shape[3:])
    return t.transpose((1, 0, 3, 2) + tuple(range(4, t.ndim)))


def from_chunks(t):
    nc, b, h, l, d = t.shape
    return t.transpose(1, 0, 3, 2, 4).reshape(b, nc * l, h, d)


def causal_dwconv(x, w):
    k, c = w.shape
    return lax.conv_general_dilated(x, w[:, None, :].astype(x.dtype), window_strides=(1,),
                                    padding=[(k - 1, 0)], dimension_numbers=("NWC", "WIO", "NWC"),
                                    feature_group_count=c)


def mlstm_mixer(h, w_in, b_gates, head_g, w_out):
    B, S, _ = h.shape
    p = (h @ w_in).astype(jnp.float32)
    q, k, v, o, ig, fg = jnp.split(p, ML_SPLITS, axis=-1)
    q = q.reshape(B, S, ML_H, ML_DQK) * (ML_DQK ** -0.5)
    k = k.reshape(B, S, ML_H, ML_DQK)
    v = v.reshape(B, S, ML_H, ML_DV)
    bg = b_gates.astype(jnp.float32)
    ig = GATE_SOFTCAP * jnp.tanh((ig + bg[:ML_H]) / GATE_SOFTCAP)
    fg = GATE_SOFTCAP * jnp.tanh((fg + bg[ML_H:]) / GATE_SOFTCAP)
    logf = jax.nn.log_sigmoid(fg)
    qc, kc, vc = to_chunks(q), to_chunks(k), to_chunks(v)
    igc, lfc = to_chunks(ig), to_chunks(logf)
    bcum = jnp.cumsum(lfc, axis=-1)
    causal = jnp.tril(jnp.ones((CHUNK, CHUNK), dtype=bool))
    dmat = jnp.where(causal, bcum[..., :, None] - bcum[..., None, :] + igc[..., None, :], -jnp.inf)
    dmax = dmat.max(-1)
    b_end = bcum[..., -1]
    w_end = b_end[..., None] - bcum + igc
    w_end_max = w_end.max(-1)

    def step(carry, xs):
        C, n, m = carry
        q_, k_, v_, b_, dm, dmx, be, we, wem = xs
        inter = b_ + m[..., None]
        mt = jnp.maximum(inter, dmx)
        a_inter = jnp.exp(inter - mt)
        pm = jnp.einsum('bhtd,bhsd->bhts', q_, k_) * jnp.exp(dm - mt[..., None])
        num = a_inter[..., None] * jnp.einsum('bhtd,bhde->bhte', q_, C) + jnp.einsum('bhts,bhse->bhte', pm, v_)
        den = a_inter * jnp.einsum('bhtd,bhd->bht', q_, n) + pm.sum(-1)
        out = num / jnp.maximum(jnp.abs(den), jnp.exp(-mt))[..., None]
        m_new = jnp.maximum(be + m, wem)
        a_state = jnp.exp(be + m - m_new)
        wk = k_ * jnp.exp(we - m_new[..., None])[..., None]
        C = a_state[..., None, None] * C + jnp.einsum('bhsd,bhse->bhde', wk, v_)
        n = a_state[..., None] * n + wk.sum(-2)
        return (C, n, m_new), out

    init = (jnp.zeros((B, ML_H, ML_DQK, ML_DV), jnp.float32),
            jnp.zeros((B, ML_H, ML_DQK), jnp.float32),
            jnp.zeros((B, ML_H), jnp.float32))
    _, hc = lax.scan(step, init, (qc, kc, vc, bcum, dmat, dmax, b_end, w_end, w_end_max))
    hs = rms_norm(from_chunks(hc), head_g.reshape(ML_H, ML_DV))
    y = hs.reshape(B, S, ML_H * ML_DV) * jax.nn.sigmoid(o)
    return y.astype(h.dtype) @ w_out


def gdn_mixer(h, w_in, conv_w, a_log, dt_bias, norm_g, w_out):
    B, S, _ = h.shape
    p = h @ w_in
    qkv, z, bt, a = jnp.split(p, GDN_SPLITS, axis=-1)
    qkv = jax.nn.silu(causal_dwconv(qkv, conv_w)).astype(jnp.float32)
    q, k, v = jnp.split(qkv, [GK_H * GD_K, 2 * GK_H * GD_K], axis=-1)
    q = l2_norm(q.reshape(B, S, GK_H, GD_K)) * (GD_K ** -0.5)
    k = l2_norm(k.reshape(B, S, GK_H, GD_K))
    q = jnp.repeat(q, GV_H // GK_H, axis=2)
    k = jnp.repeat(k, GV_H // GK_H, axis=2)
    v = v.reshape(B, S, GV_H, GD_V)
    beta = jax.nn.sigmoid(bt.astype(jnp.float32))
    g = -jnp.exp(a_log.astype(jnp.float32)) * jax.nn.softplus(a.astype(jnp.float32) + dt_bias.astype(jnp.float32))
    qc, kc, vc = to_chunks(q), to_chunks(k), to_chunks(v)
    betac, gam = to_chunks(beta), jnp.cumsum(to_chunks(g), axis=-1)
    diff = gam[..., :, None] - gam[..., None, :]
    strict = jnp.tril(jnp.ones((CHUNK, CHUNK), dtype=bool), -1)
    causal = jnp.tril(jnp.ones((CHUNK, CHUNK), dtype=bool))
    kb = kc * betac[..., None]
    vb = vc * betac[..., None]
    A = jnp.einsum('...id,...jd->...ij', kb, kc) * jnp.exp(jnp.where(strict, diff, -jnp.inf))
    U = lax.linalg.triangular_solve(A, vb, left_side=True, lower=True, unit_diagonal=True)
    W = lax.linalg.triangular_solve(A, kb * jnp.exp(gam)[..., None], left_side=True, lower=True, unit_diagonal=True)
    Aqk = jnp.einsum('...id,...jd->...ij', qc, kc) * jnp.exp(jnp.where(causal, diff, -jnp.inf))
    qg = qc * jnp.exp(gam)[..., None]
    kd = kc * jnp.exp(gam[..., -1:] - gam)[..., None]
    dl = jnp.exp(gam[..., -1])

    def step(Sst, xs):
        u_, w_, qg_, kd_, aqk_, dl_ = xs
        vn = u_ - jnp.einsum('bhld,bhde->bhle', w_, Sst)
        o = jnp.einsum('bhld,bhde->bhle', qg_, Sst) + jnp.einsum('bhls,bhse->bhle', aqk_, vn)
        Sst = dl_[..., None, None] * Sst + jnp.einsum('bhld,bhle->bhde', kd_, vn)
        return Sst, o

    S0 = jnp.zeros((B, GV_H, GD_K, GD_V), jnp.float32)
    _, oc = lax.scan(step, S0, (U, W, qg, kd, Aqk, dl))
    o = rms_norm(from_chunks(oc), norm_g) * jax.nn.silu(z.astype(jnp.float32).reshape(B, S, GV_H, GD_V))
    return o.reshape(B, S, GV_H * GD_V).astype(h.dtype) @ w_out


def mem_cross_attn(h, k_m, v_m, w_q, w_o):
    B, S, _ = h.shape
    q = (h @ w_q).reshape(B, S, XA_H, XA_DH)
    s = jnp.einsum('bshd,bmhd->bhsm', q, k_m).astype(jnp.float32) * (XA_DH ** -0.5)
    pr = jax.nn.softmax(s, axis=-1).astype(h.dtype)
    o = jnp.einsum('bhsm,bmhd->bshd', pr, v_m).reshape(B, S, XA_H * XA_DH)
    return o @ w_o


def sq_relu_mlp(h, w_up, w_down):
    u = jax.nn.relu(h @ w_up)
    return (u * u) @ w_down


def setup_inputs(seed: int = 0) -> dict:
    key = jax.random.key(seed)
    ks = jax.random.split(key, 24)
    f32 = jnp.float32

    def nrm(k, shape, fan_in):
        return jax.random.normal(k, shape, f32) * (fan_in ** -0.5)

    def gain(k, shape):
        return 1.0 + 0.02 * jax.random.normal(k, shape, f32)

    dt = jnp.exp(jax.random.uniform(ks[20], (N_GDN, GV_H), f32) * (math.log(0.1) - math.log(0.001)) + math.log(0.001))
    return {
        "x": jax.random.normal(ks[0], (BATCH, SEQ, D_MODEL), f32),
        "mem": jax.random.normal(ks[1], (BATCH, MEM_LEN, D_MODEL), f32),
        "norm_g": gain(ks[2], (DEPTH, 6, D_MODEL)),
        "mem_norm_g": gain(ks[3], (D_MODEL,)),
        "w_mem_kv": nrm(ks[4], (D_MODEL, 2 * XA_H * XA_DH), D_MODEL),
        "w_xq": nrm(ks[5], (DEPTH, D_MODEL, XA_H * XA_DH), D_MODEL),
        "w_xo": nrm(ks[6], (DEPTH, XA_H * XA_DH, D_MODEL), XA_H * XA_DH),
        "w_up": nrm(ks[7], (DEPTH, D_MODEL, D_FF), D_MODEL),
        "w_down": nrm(ks[8], (DEPTH, D_FF, D_MODEL), D_FF),
        "mlstm_w_in": nrm(ks[9], (N_MLSTM, D_MODEL, ML_IN), D_MODEL),
        "mlstm_b_gates": jnp.concatenate([0.1 * jax.random.normal(ks[10], (N_MLSTM, ML_H), f32),
                                          3.0 + 3.0 * jax.random.uniform(ks[11], (N_MLSTM, ML_H), f32)], axis=-1),
        "mlstm_head_g": gain(ks[12], (N_MLSTM, ML_H * ML_DV)),
        "mlstm_w_out": nrm(ks[13], (N_MLSTM, ML_H * ML_DV, D_MODEL), ML_H * ML_DV),
        "gdn_w_in": nrm(ks[14], (N_GDN, D_MODEL, GDN_IN), D_MODEL),
        "gdn_conv_w": nrm(ks[15], (N_GDN, CONV_K, GDN_QKV), CONV_K),
        "gdn_a_log": jnp.log(jax.random.uniform(ks[16], (N_GDN, GV_H), f32, 1.0, 16.0)),
        "gdn_dt_bias": dt + jnp.log(-jnp.expm1(-dt)),
        "gdn_norm_g": gain(ks[17], (N_GDN, GD_V)),
        "gdn_w_out": nrm(ks[18], (N_GDN, GV_H * GD_V, D_MODEL), GV_H * GD_V),
    }


def reference(x, mem, norm_g, mem_norm_g, w_mem_kv, w_xq, w_xo, w_up, w_down,
              mlstm_w_in, mlstm_b_gates, mlstm_head_g, mlstm_w_out,
              gdn_w_in, gdn_conv_w, gdn_a_log, gdn_dt_bias, gdn_norm_g, gdn_w_out):
    B = mem.shape[0]
    kv = (rms_norm(mem, mem_norm_g) @ w_mem_kv).reshape(B, MEM_LEN, 2, XA_H, XA_DH)
    k_m, v_m = kv[:, :, 0], kv[:, :, 1]
    for i in range(DEPTH):
        g = norm_g[i]
        j = i // N_MIXERS
        h = rms_norm(x, g[0])
        if i % N_MIXERS == 0:
            mix = mlstm_mixer(h, mlstm_w_in[j], mlstm_b_gates[j], mlstm_head_g[j], mlstm_w_out[j])
        else:
            mix = gdn_mixer(h, gdn_w_in[j], gdn_conv_w[j], gdn_a_log[j], gdn_dt_bias[j], gdn_norm_g[j], gdn_w_out[j])
        x = x + rms_norm(mix, g[1])
        h = rms_norm(x, g[2])
        x = x + rms_norm(mem_cross_attn(h, k_m, v_m, w_xq[i], w_xo[i]), g[3])
        h = rms_norm(x, g[4])
        x = x + rms_norm(sq_relu_mlp(h, w_up[i], w_down[i]), g[5])
    return x
```

```python
import functools

import jax
import jax.numpy as jnp
from jax import lax
from jax.experimental import pallas as pl
from jax.experimental.pallas import tpu as pltpu

EPS = 1e-6
GATE_SOFTCAP = 15.0
D_MODEL = 2048
ML_H = 4
ML_DV = D_MODEL // ML_H
ML_DQK = ML_DV // 2
ML_CHUNK = 256
GK_H = 16
GV_H = 32
GD = 128
CONV_K = 4
GDN_CHUNK = 64
GDN_HEADS_PER_STEP = 4
GDN_ROWS_PER_STEP = 128
GDN_INV_BASE = 8
MEM_LEN = 256
XA_H = 4
XA_DH = 128
D_FF = 4 * D_MODEL

LANES = 128
SUBLANES = 8
NEG_BIG = -1e30
VMEM_LIMIT = 56 * 1024 * 1024

BF16 = jnp.bfloat16
F32 = jnp.float32


def _params(sem):
    return pltpu.CompilerParams(dimension_semantics=sem, vmem_limit_bytes=VMEM_LIMIT)


def _rms(xf, g):
    ms = jnp.mean(xf * xf, axis=-1, keepdims=True)
    return xf * lax.rsqrt(ms + EPS) * g


def _dot(a, b):
    return jnp.dot(a, b, preferred_element_type=F32)


def _dot_nt(a, b):
    return lax.dot_general(a, b, (((1,), (1,)), ((), ())), preferred_element_type=F32)


def _dot_tn(a, b):
    return lax.dot_general(a, b, (((0,), (0,)), ((), ())), preferred_element_type=F32)


def _split3(x):
    hi = x.astype(BF16)
    r = x - hi.astype(F32)
    mid = r.astype(BF16)
    lo = (r - mid.astype(F32)).astype(BF16)
    return hi, mid, lo


def _cumsum_rows(tri, x):
    hi, mid, lo = _split3(x)
    return _dot(tri, hi) + _dot(tri, mid) + _dot(tri, lo)


def _cumsum_cols(x, tri):
    hi, mid, lo = _split3(x)
    return _dot(hi, tri) + _dot(mid, tri) + _dot(lo, tri)


def _softplus(x):
    return jnp.maximum(x, 0.0) + jnp.log1p(jnp.exp(-jnp.abs(x)))


def _sigmoid(x):
    return 1.0 / (1.0 + jnp.exp(-x))


def _norm_matmul_kernel(x_ref, g_ref, w_ref, o_ref, h_ref):
    @pl.when(pl.program_id(1) == 0)
    def _():
        h_ref[...] = _rms(x_ref[...], g_ref[...]).astype(BF16)

    o_ref[...] = _dot(h_ref[...], w_ref[...]).astype(o_ref.dtype)


def norm_matmul(x, g, w, out_dtype, tm, tn):
    t, d = x.shape
    n = w.shape[1]
    tm = min(tm, t)
    tn = min(tn, n)
    return pl.pallas_call(
        _norm_matmul_kernel,
        grid=(t // tm, n // tn),
        in_specs=[
            pl.BlockSpec((tm, d), lambda i, j: (i, 0)),
            pl.BlockSpec((1, d), lambda i, j: (0, 0)),
            pl.BlockSpec((d, tn), lambda i, j: (0, j)),
        ],
        out_specs=pl.BlockSpec((tm, tn), lambda i, j: (i, j)),
        out_shape=jax.ShapeDtypeStruct((t, n), out_dtype),
        scratch_shapes=[pltpu.VMEM((tm, d), BF16)],
        compiler_params=_params(("parallel", "arbitrary")),
        name="norm_matmul",
    )(x, g.reshape(1, d), w)


def _matmul_norm_res_kernel(y_ref, w_ref, g_ref, x_ref, o_ref, acc_ref):
    k = pl.program_id(1)

    @pl.when(k == 0)
    def _():
        acc_ref[...] = jnp.zeros_like(acc_ref)

    acc_ref[...] += _dot(y_ref[...], w_ref[...])

    @pl.when(k == pl.num_programs(1) - 1)
    def _():
        o_ref[...] = x_ref[...] + _rms(acc_ref[...], g_ref[...])


def matmul_norm_res(y, w, g, x, tm, tk):
    t, kdim = y.shape
    d = w.shape[1]
    tm = min(tm, t)
    tk = min(tk, kdim)
    return pl.pallas_call(
        _matmul_norm_res_kernel,
        grid=(t // tm, kdim // tk),
        in_specs=[
            pl.BlockSpec((tm, tk), lambda i, k: (i, k)),
            pl.BlockSpec((tk, d), lambda i, k: (k, 0)),
            pl.BlockSpec((1, d), lambda i, k: (0, 0)),
            pl.BlockSpec((tm, d), lambda i, k: (i, 0)),
        ],
        out_specs=pl.BlockSpec((tm, d), lambda i, k: (i, 0)),
        out_shape=jax.ShapeDtypeStruct((t, d), F32),
        scratch_shapes=[pltpu.VMEM((tm, d), F32)],
        compiler_params=_params(("parallel", "arbitrary")),
        name="matmul_norm_res",
    )(y, w, g.reshape(1, d), x)


def _mem_kv_kernel(m_ref, g_ref, w_ref, o_ref):
    o_ref[...] = _dot(_rms(m_ref[...], g_ref[...]).astype(BF16), w_ref[...]).astype(o_ref.dtype)


def mem_kv(mem2d, g, w):
    t, d = mem2d.shape
    n = w.shape[1]
    tm = min(256, t)
    return pl.pallas_call(
        _mem_kv_kernel,
        grid=(t // tm,),
        in_specs=[
            pl.BlockSpec((tm, d), lambda i: (i, 0)),
            pl.BlockSpec((1, d), lambda i: (0, 0)),
            pl.BlockSpec((d, n), lambda i: (0, 0)),
        ],
        out_specs=pl.BlockSpec((tm, n), lambda i: (i, 0)),
        out_shape=jax.ShapeDtypeStruct((t, n), BF16),
        compiler_params=_params(("parallel",)),
        name="mem_kv",
    )(mem2d, g.reshape(1, d), w)


def _xattn_kernel(x_ref, g_pre_ref, wq_ref, kv_ref, wo_ref, g_post_ref, o_ref):
    x = x_ref[...]
    h = _rms(x, g_pre_ref[...]).astype(BF16)
    q = _dot(h, wq_ref[...]).astype(BF16)
    kv = kv_ref[...]
    heads = []
    for a in range(XA_H):
        qa = q[:, a * XA_DH:(a + 1) * XA_DH]
        ka = kv[:, a * XA_DH:(a + 1) * XA_DH]
        va = kv[:, (XA_H + a) * XA_DH:(XA_H + a + 1) * XA_DH]
        s = _dot_nt(qa, ka) * (XA_DH ** -0.5)
        s = s - jnp.max(s, axis=-1, keepdims=True)
        e = jnp.exp(s)
        p = e / jnp.sum(e, axis=-1, keepdims=True)
        heads.append(_dot(p.astype(BF16), va).astype(BF16))
    o = jnp.concatenate(heads, axis=-1)
    o_ref[...] = x + _rms(_dot(o, wo_ref[...]), g_post_ref[...])


def xattn(x, g_pre, wq, kv, wo, g_post, seq, tm):
    t, d = x.shape
    tm = min(tm, seq)
    blocks_per_batch = seq // tm
    nq = wq.shape[1]
    return pl.pallas_call(
        _xattn_kernel,
        grid=(t // tm,),
        in_specs=[
            pl.BlockSpec((tm, d), lambda i: (i, 0)),
            pl.BlockSpec((1, d), lambda i: (0, 0)),
            pl.BlockSpec((d, nq), lambda i: (0, 0)),
            pl.BlockSpec((MEM_LEN, 2 * nq), lambda i: (i // blocks_per_batch, 0)),
            pl.BlockSpec((nq, d), lambda i: (0, 0)),
            pl.BlockSpec((1, d), lambda i: (0, 0)),
        ],
        out_specs=pl.BlockSpec((tm, d), lambda i: (i, 0)),
        out_shape=jax.ShapeDtypeStruct((t, d), F32),
        compiler_params=_params(("parallel",)),
        name="xattn",
    )(x, g_pre.reshape(1, d), wq, kv, wo, g_post.reshape(1, d))


def _mlp_kernel(x_ref, g_pre_ref, wu_ref, wd_ref, g_post_ref, o_ref, h_ref, acc_ref):
    j = pl.program_id(1)

    @pl.when(j == 0)
    def _():
        h_ref[...] = _rms(x_ref[...], g_pre_ref[...]).astype(BF16)
        acc_ref[...] = jnp.zeros_like(acc_ref)

    u = jnp.maximum(_dot(h_ref[...], wu_ref[...]), 0.0)
    acc_ref[...] += _dot((u * u).astype(BF16), wd_ref[...])

    @pl.when(j == pl.num_programs(1) - 1)
    def _():
        o_ref[...] = x_ref[...] + _rms(acc_ref[...], g_post_ref[...])


def mlp(x, g_pre, wu, wd, g_post, tm, tf):
    t, d = x.shape
    f = wu.shape[1]
    tm = min(tm, t)
    return pl.pallas_call(
        _mlp_kernel,
        grid=(t // tm, f // tf),
        in_specs=[
            pl.BlockSpec((tm, d), lambda i, j: (i, 0)),
            pl.BlockSpec((1, d), lambda i, j: (0, 0)),
            pl.BlockSpec((d, tf), lambda i, j: (0, j)),
            pl.BlockSpec((tf, d), lambda i, j: (j, 0)),
            pl.BlockSpec((1, d), lambda i, j: (0, 0)),
        ],
        out_specs=pl.BlockSpec((tm, d), lambda i, j: (i, 0)),
        out_shape=jax.ShapeDtypeStruct((t, d), F32),
        scratch_shapes=[pltpu.VMEM((tm, d), BF16), pltpu.VMEM((tm, d), F32)],
        compiler_params=_params(("parallel", "arbitrary")),
        name="mlp",
    )(x, g_pre.reshape(1, d), wu, wd, g_post.reshape(1, d))


def _mlstm_kernel(p_ref, gc_ref, gr_ref, bc_ref, br_ref, hg_ref, y_ref, c_ref, n_ref, m_ref):
    L = p_ref.shape[0]

    @pl.when(pl.program_id(1) == 0)
    def _():
        c_ref[...] = jnp.zeros_like(c_ref)
        n_ref[...] = jnp.zeros_like(n_ref)
        m_ref[...] = jnp.zeros_like(m_ref)

    row = lax.broadcasted_iota(jnp.int32, (L, L), 0)
    col = lax.broadcasted_iota(jnp.int32, (L, L), 1)
    causal = row >= col
    tril = causal.astype(BF16)
    triu = (row <= col).astype(BF16)

    gc = GATE_SOFTCAP * jnp.tanh((gc_ref[...] + bc_ref[...]) / GATE_SOFTCAP)
    lfc = -_softplus(-gc)
    bcum_c_all = _cumsum_rows(tril, lfc)
    gr = GATE_SOFTCAP * jnp.tanh((gr_ref[...] + br_ref[...]) / GATE_SOFTCAP)
    lfr = -_softplus(-gr)
    bcum_r_all = _cumsum_cols(lfr, triu)

    qo, ko, vo, oo = 0, ML_H * ML_DQK, 2 * ML_H * ML_DQK, 2 * ML_H * ML_DQK + ML_H * ML_DV
    for h in range(ML_H):
        q = p_ref[:, qo + h * ML_DQK: qo + (h + 1) * ML_DQK] * (ML_DQK ** -0.5)
        k = p_ref[:, ko + h * ML_DQK: ko + (h + 1) * ML_DQK]
        v = p_ref[:, vo + h * ML_DV: vo + (h + 1) * ML_DV]
        og = p_ref[:, oo + h * ML_DV: oo + (h + 1) * ML_DV].astype(F32)

        ig_c = gc[:, h:h + 1]
        ig_r = gr[h:h + 1, :]
        bcum_c = bcum_c_all[:, ML_H + h: ML_H + h + 1]
        bcum_r = bcum_r_all[ML_H + h: ML_H + h + 1, :]
        b_end = bcum_r[:, L - 1:L]
        m_prev = m_ref[h]
        c_prev = c_ref[h]
        n_prev = n_ref[h]

        dmat = jnp.where(causal, bcum_c - bcum_r + ig_r, NEG_BIG)
        dmx = jnp.max(dmat, axis=-1, keepdims=True)
        inter = bcum_c + m_prev
        mt = jnp.maximum(inter, dmx)
        a_inter = jnp.exp(inter - mt)
        pm = _dot_nt(q, k) * jnp.exp(dmat - mt)
        num = a_inter * _dot(q, c_prev.astype(BF16)) + _dot(pm.astype(BF16), v)
        den = (a_inter * jnp.sum(q.astype(F32) * n_prev, axis=-1, keepdims=True)
               + jnp.sum(pm, axis=-1, keepdims=True))
        out = num * (1.0 / jnp.maximum(jnp.abs(den), jnp.exp(-mt)))

        w_end_r = b_end - bcum_r + ig_r
        m_new = jnp.maximum(b_end + m_prev, jnp.max(w_end_r, axis=-1, keepdims=True))
        a_state = jnp.exp(b_end + m_prev - m_new)
        wk = k.astype(F32) * jnp.exp(b_end - bcum_c + ig_c - m_new)
        c_ref[h] = a_state * c_prev + _dot_tn(wk.astype(BF16), v)
        n_ref[h] = a_state * n_prev + jnp.sum(wk, axis=0, keepdims=True)
        m_ref[h] = m_new

        hs = _rms(out, hg_ref[:, h * ML_DV:(h + 1) * ML_DV])
        y_ref[:, h * ML_DV:(h + 1) * ML_DV] = (hs * _sigmoid(og)).astype(y_ref.dtype)


def mlstm_scan(p, gates_col, gates_row, bias_col, bias_row, head_g, batch, seq):
    t = p.shape[0]
    L = min(ML_CHUNK, seq)
    nc = seq // L
    return pl.pallas_call(
        _mlstm_kernel,
        grid=(batch, nc),
        in_specs=[
            pl.BlockSpec((L, p.shape[1]), lambda b, c: (b * nc + c, 0)),
            pl.BlockSpec((L, LANES), lambda b, c: (b * nc + c, 0)),
            pl.BlockSpec((2 * ML_H, L), lambda b, c: (0, b * nc + c)),
            pl.BlockSpec((1, LANES), lambda b, c: (0, 0)),
            pl.BlockSpec((2 * ML_H, 1), lambda b, c: (0, 0)),
            pl.BlockSpec((1, ML_H * ML_DV), lambda b, c: (0, 0)),
        ],
        out_specs=pl.BlockSpec((L, ML_H * ML_DV), lambda b, c: (b * nc + c, 0)),
        out_shape=jax.ShapeDtypeStruct((t, ML_H * ML_DV), BF16),
        scratch_shapes=[
            pltpu.VMEM((ML_H, ML_DQK, ML_DV), F32),
            pltpu.VMEM((ML_H, 1, ML_DQK), F32),
            pltpu.VMEM((ML_H, 1, 1), F32),
        ],
        compiler_params=_params(("parallel", "arbitrary")),
        name="mlstm_scan",
    )(p, gates_col, gates_row, bias_col, bias_row, head_g)


def _gdn_conv_kernel(x_ref, halo_ref, w_ref, o_ref, *, blocks_per_batch, n_q_blocks, n_qk_blocks):
    i = pl.program_id(0)
    j = pl.program_id(1)
    tm, tc = x_ref.shape
    x = x_ref[...].astype(F32)
    halo = halo_ref[...].astype(F32)
    halo = jnp.where(i % blocks_per_batch == 0, jnp.zeros_like(halo), halo)
    xp = jnp.concatenate([halo, x], axis=0)
    w = w_ref[...]
    acc = xp[SUBLANES:, :] * w[CONV_K - 1:CONV_K, :]
    for s in range(1, CONV_K):
        acc = acc + xp[SUBLANES - s: SUBLANES - s + tm, :] * w[CONV_K - 1 - s: CONV_K - s, :]
    y = acc * _sigmoid(acc)

    @pl.when(j < n_qk_blocks)
    def _():
        scale = jnp.where(j < n_q_blocks, GD ** -0.5, 1.0)
        for c in range(tc // GD):
            blk = y[:, c * GD:(c + 1) * GD]
            ss = jnp.sum(blk * blk, axis=-1, keepdims=True)
            o_ref[:, c * GD:(c + 1) * GD] = (blk * (lax.rsqrt(ss + EPS) * scale)).astype(o_ref.dtype)

    @pl.when(j >= n_qk_blocks)
    def _():
        o_ref[...] = y.astype(o_ref.dtype)


def gdn_conv(p, conv_w, seq, tm, tc):
    t = p.shape[0]
    nqkv = conv_w.shape[1]
    tm = min(tm, seq)
    blocks_per_batch = seq // tm
    halo_blocks = tm // SUBLANES
    kern = functools.partial(
        _gdn_conv_kernel, blocks_per_batch=blocks_per_batch,
        n_q_blocks=GK_H * GD // tc, n_qk_blocks=2 * GK_H * GD // tc)
    return pl.pallas_call(
        kern,
        grid=(t // tm, nqkv // tc),
        in_specs=[
            pl.BlockSpec((tm, tc), lambda i, j: (i, j)),
            pl.BlockSpec((SUBLANES, tc), lambda i, j: (jnp.maximum(i * halo_blocks - 1, 0), j)),
            pl.BlockSpec((CONV_K, tc), lambda i, j: (0, j)),
        ],
        out_specs=pl.BlockSpec((tm, tc), lambda i, j: (i, j)),
        out_shape=jax.ShapeDtypeStruct((t, nqkv), BF16),
        compiler_params=_params(("parallel", "parallel")),
        name="gdn_conv",
    )(p, p, conv_w)


def _unit_lower_inverse(a, eye, same_block):
    L = a.shape[0]
    d = jnp.where(same_block[GDN_INV_BASE], a, 0.0)
    pw = (-d).astype(BF16)
    t = eye - d
    for _ in range(GDN_INV_BASE.bit_length() - 2):
        pw = _dot(pw, pw).astype(BF16)
        t = t + _dot(t.astype(BF16), pw)
    size = GDN_INV_BASE
    while size < L:
        off = jnp.where(same_block[2 * size] & jnp.logical_not(same_block[size]), a, 0.0)
        x = _dot(off.astype(BF16), t.astype(BF16))
        t = t - _dot(t.astype(BF16), x.astype(BF16))
        size *= 2
    return t


def _gdn_kernel(q_ref, k_ref, v_ref, z_ref, gc_ref, gr_ref, pc_ref, pr_ref, ng_ref, y_ref, s_ref):
    G = GDN_HEADS_PER_STEP
    L = GDN_CHUNK
    rows = q_ref.shape[0]

    @pl.when(pl.program_id(2) == 0)
    def _():
        s_ref[...] = jnp.zeros_like(s_ref)

    row = lax.broadcasted_iota(jnp.int32, (L, L), 0)
    col = lax.broadcasted_iota(jnp.int32, (L, L), 1)
    causal = row >= col
    strict = row > col
    eye = (row == col).astype(F32)
    tril = causal.astype(BF16)
    triu = (row <= col).astype(BF16)
    same_block = {}
    size = GDN_INV_BASE
    while size <= L:
        shift = size.bit_length() - 1
        same_block[size] = (row >> shift) == (col >> shift)
        size *= 2

    gcol = gc_ref[...]
    pc = pc_ref[...]
    beta_c_all = _sigmoid(gcol)
    g_c_all = -jnp.exp(pc[0:1, :]) * _softplus(gcol + pc[1:2, :])
    grow = gr_ref[...]
    pr = pr_ref[...]
    g_r_all = -jnp.exp(pr[:, 0:1]) * _softplus(grow + pr[:, 1:2])

    for cc in range(rows // L):
        r0 = cc * L
        gam_c_all = _cumsum_rows(tril, g_c_all[r0:r0 + L, :])
        gam_r_all = _cumsum_cols(g_r_all[:, r0:r0 + L], triu)
        for kh in range(G // 2):
            q = q_ref[r0:r0 + L, kh * GD:(kh + 1) * GD]
            k = k_ref[r0:r0 + L, kh * GD:(kh + 1) * GD]
            kt = k.astype(F32).T
            kk = _dot(k, kt.astype(BF16))
            qk = _dot(q, kt.astype(BF16))
            kf = k.astype(F32)
            qf = q.astype(F32)
            for jj in range(2):
                j = 2 * kh + jj
                v = v_ref[r0:r0 + L, j * GD:(j + 1) * GD].astype(F32)
                z = z_ref[r0:r0 + L, j * GD:(j + 1) * GD].astype(F32)
                beta = beta_c_all[r0:r0 + L, j:j + 1]
                gam_c = gam_c_all[:, G + j:G + j + 1]
                gam_r = gam_r_all[G + j:G + j + 1, :]
                gam_last = gam_r[:, L - 1:L]
                e = jnp.exp(jnp.where(causal, gam_c - gam_r, NEG_BIG))
                a = beta * kk * jnp.where(strict, e, 0.0)
                aqk = qk * e

                tinv = _unit_lower_inverse(a, eye, same_block)

                egam = jnp.exp(gam_c)
                rhs = jnp.concatenate([v * beta, kf * (beta * egam)], axis=-1).astype(BF16)
                uw = _dot(tinv.astype(BF16), rhs)
                u = uw[:, :GD]
                w = uw[:, GD:]

                s_prev = s_ref[j]
                s_bf = s_prev.astype(BF16)
                lhs = jnp.concatenate([w.astype(BF16), (qf * egam).astype(BF16)], axis=0)
                ws_qs = _dot(lhs, s_bf)
                vn = u - ws_qs[:L, :]
                vn_bf = vn.astype(BF16)
                o = ws_qs[L:, :] + _dot(aqk.astype(BF16), vn_bf)
                kdt = (kt * jnp.exp(gam_last - gam_r)).astype(BF16)
                s_ref[j] = jnp.exp(gam_last) * s_prev + _dot(kdt, vn_bf)

                yv = _rms(o, ng_ref[...]) * (z * _sigmoid(z))
                y_ref[r0:r0 + L, j * GD:(j + 1) * GD] = yv.astype(y_ref.dtype)


def gdn_scan(qkv, z_src, z_col0, gates_col, gates_row, par_col, par_row, norm_g, batch, seq):
    t = qkv.shape[0]
    G = GDN_HEADS_PER_STEP
    rows = min(GDN_ROWS_PER_STEP, seq)
    nrb = seq // rows
    ng = GV_H // G
    qw = (G // 2) * GD
    vw = G * GD
    k_blk0 = GK_H * GD // qw
    v_blk0 = 2 * GK_H * GD // vw
    z_blk0 = z_col0 // vw
    return pl.pallas_call(
        _gdn_kernel,
        grid=(batch, ng, nrb),
        in_specs=[
            pl.BlockSpec((rows, qw), lambda b, g, c: (b * nrb + c, g)),
            pl.BlockSpec((rows, qw), lambda b, g, c: (b * nrb + c, k_blk0 + g)),
            pl.BlockSpec((rows, vw), lambda b, g, c: (b * nrb + c, v_blk0 + g)),
            pl.BlockSpec((rows, vw), lambda b, g, c: (b * nrb + c, z_blk0 + g)),
            pl.BlockSpec((None, rows, 2 * G), lambda b, g, c: (g, b * nrb + c, 0)),
            pl.BlockSpec((None, 2 * G, rows), lambda b, g, c: (g, 0, b * nrb + c)),
            pl.BlockSpec((None, 2, 2 * G), lambda b, g, c: (g, 0, 0)),
            pl.BlockSpec((None, 2 * G, 2), lambda b, g, c: (g, 0, 0)),
            pl.BlockSpec((1, GD), lambda b, g, c: (0, 0)),
        ],
        out_specs=pl.BlockSpec((rows, vw), lambda b, g, c: (b * nrb + c, g)),
        out_shape=jax.ShapeDtypeStruct((t, GV_H * GD), BF16),
        scratch_shapes=[pltpu.VMEM((G, GD, GD), F32)],
        compiler_params=_params(("parallel", "parallel", "arbitrary")),
        name="gdn_scan",
    )(qkv, qkv, qkv, z_src, gates_col, gates_row, par_col, par_row, norm_g.reshape(1, GD))


def _pad_cols(w, n):
    return jnp.pad(w, ((0, 0), (0, n - w.shape[1])))


def mlstm_layer(x, g_pre, g_post, w_in, b_gates, head_g, w_out, batch, seq):
    n_main = 2 * ML_H * ML_DQK + 2 * ML_H * ML_DV
    w_main = w_in[:, :n_main].astype(BF16)
    w_gate = _pad_cols(w_in[:, n_main:], LANES).astype(BF16)
    p = norm_matmul(x, g_pre, w_main, BF16, 1024, 512)
    gates_col = norm_matmul(x, g_pre, w_gate, F32, 1024, LANES)
    gates_row = gates_col[:, :2 * ML_H].T
    bias_col = _pad_cols(b_gates.reshape(1, -1), LANES)
    bias_row = b_gates.reshape(-1, 1)
    y = mlstm_scan(p, gates_col, gates_row, bias_col, bias_row, head_g.reshape(1, -1), batch, seq)
    return matmul_norm_res(y, w_out.astype(BF16), g_post, x, 512, 1024)


def gdn_layer(x, g_pre, g_post, w_in, conv_w, a_log, dt_bias, norm_g, w_out, batch, seq):
    G = GDN_HEADS_PER_STEP
    ng = GV_H // G
    t = x.shape[0]
    n_qkv = 2 * GK_H * GD + GV_H * GD
    n_main = n_qkv + GV_H * GD
    w_main = w_in[:, :n_main].astype(BF16)
    w_gate = _pad_cols(w_in[:, n_main:], LANES).astype(BF16)
    p = norm_matmul(x, g_pre, w_main, BF16, 1024, 512)
    gates = norm_matmul(x, g_pre, w_gate, F32, 1024, LANES)[:, :2 * GV_H]
    gates_col = gates.reshape(t, 2, ng, G).transpose(2, 0, 1, 3).reshape(ng, t, 2 * G)
    gates_row = gates_col.transpose(0, 2, 1)
    zeros = jnp.zeros((ng, G), F32)
    a_grp = jnp.concatenate([zeros, a_log.reshape(ng, G)], axis=-1)
    dt_grp = jnp.concatenate([zeros, dt_bias.reshape(ng, G)], axis=-1)
    par_col = jnp.stack([a_grp, dt_grp], axis=1)
    par_row = par_col.transpose(0, 2, 1)
    qkv = gdn_conv(p, conv_w, seq, 512, 512)
    y = gdn_scan(qkv, p, n_qkv, gates_col, gates_row, par_col, par_row, norm_g, batch, seq)
    return matmul_norm_res(y, w_out.astype(BF16), g_post, x, 512, 1024)


def kernel(x, mem, norm_g, mem_norm_g, w_mem_kv, w_xq, w_xo, w_up, w_down, mlstm_w_in, mlstm_b_gates, mlstm_head_g, mlstm_w_out, gdn_w_in, gdn_conv_w, gdn_a_log, gdn_dt_bias, gdn_norm_g, gdn_w_out):
    batch, seq, d = x.shape
    depth = norm_g.shape[0]
    kv = mem_kv(mem.reshape(batch * MEM_LEN, d), mem_norm_g, w_mem_kv.astype(BF16))
    xt = x.reshape(batch * seq, d)
    for i in range(depth):
        g = norm_g[i]
        j = i // 2
        if i % 2 == 0:
            xt = mlstm_layer(xt, g[0], g[1], mlstm_w_in[j], mlstm_b_gates[j], mlstm_head_g[j],
                             mlstm_w_out[j], batch, seq)
        else:
            xt = gdn_layer(xt, g[0], g[1], gdn_w_in[j], gdn_conv_w[j], gdn_a_log[j], gdn_dt_bias[j],
                           gdn_norm_g[j], gdn_w_out[j], batch, seq)
        xt = xattn(xt, g[2], w_xq[i].astype(BF16), kv, w_xo[i].astype(BF16), g[3], seq, 512)
        xt = mlp(xt, g[4], w_up[i].astype(BF16), w_down[i].astype(BF16), g[5], 512, 512)
    return xt.reshape(batch, seq, d)
```

```python
import functools

import jax
import jax.numpy as jnp
from jax import lax
from jax.experimental import pallas as pl
from jax.experimental.pallas import tpu as pltpu

EPS = 1e-6
GATE_SOFTCAP = 15.0
D_MODEL = 2048
ML_H = 4
ML_DV = D_MODEL // ML_H
ML_DQK = ML_DV // 2
ML_CHUNK = 256
GK_H = 16
GV_H = 32
GD = 128
CONV_K = 4
GDN_CHUNK = 64
GDN_HEADS_PER_STEP = 8
GDN_ROWS_PER_STEP = 256
GDN_INV_BASE = 8
MEM_LEN = 256
XA_H = 4
XA_DH = 128
D_FF = 4 * D_MODEL

LANES = 128
SUBLANES = 8
NEG_BIG = -1e30
VMEM_LIMIT = 56 * 1024 * 1024

BF16 = jnp.bfloat16
F32 = jnp.float32


def _params(sem):
    return pltpu.CompilerParams(dimension_semantics=sem, vmem_limit_bytes=VMEM_LIMIT)


def _rms(xf, g):
    ms = jnp.mean(xf * xf, axis=-1, keepdims=True)
    return xf * lax.rsqrt(ms + EPS) * g


def _dot(a, b):
    return jnp.dot(a, b, preferred_element_type=F32)


def _dot_nt(a, b):
    return lax.dot_general(a, b, (((1,), (1,)), ((), ())), preferred_element_type=F32)


def _dot_tn(a, b):
    return lax.dot_general(a, b, (((0,), (0,)), ((), ())), preferred_element_type=F32)


def _split3(x):
    hi = x.astype(BF16)
    r = x - hi.astype(F32)
    mid = r.astype(BF16)
    lo = (r - mid.astype(F32)).astype(BF16)
    return hi, mid, lo


def _cumsum_rows(tri, x):
    hi, mid, lo = _split3(x)
    return _dot(tri, hi) + _dot(tri, mid) + _dot(tri, lo)


def _cumsum_cols(x, tri):
    hi, mid, lo = _split3(x)
    return _dot(hi, tri) + _dot(mid, tri) + _dot(lo, tri)


def _softplus(x):
    return jnp.maximum(x, 0.0) + jnp.log1p(jnp.exp(-jnp.abs(x)))


def _sigmoid(x):
    return 1.0 / (1.0 + jnp.exp(-x))


def _norm_matmul_kernel(x_ref, g_ref, w_ref, o_ref, h_ref):
    @pl.when(pl.program_id(1) == 0)
    def _():
        h_ref[...] = _rms(x_ref[...], g_ref[...]).astype(BF16)

    o_ref[...] = _dot(h_ref[...], w_ref[...]).astype(o_ref.dtype)


def norm_matmul(x, g, w, out_dtype, tm, tn):
    t, d = x.shape
    n = w.shape[1]
    tm = min(tm, t)
    tn = min(tn, n)
    return pl.pallas_call(
        _norm_matmul_kernel,
        grid=(t // tm, n // tn),
        in_specs=[
            pl.BlockSpec((tm, d), lambda i, j: (i, 0)),
            pl.BlockSpec((1, d), lambda i, j: (0, 0)),
            pl.BlockSpec((d, tn), lambda i, j: (0, j)),
        ],
        out_specs=pl.BlockSpec((tm, tn), lambda i, j: (i, j)),
        out_shape=jax.ShapeDtypeStruct((t, n), out_dtype),
        scratch_shapes=[pltpu.VMEM((tm, d), BF16)],
        compiler_params=_params(("parallel", "arbitrary")),
        name="norm_matmul",
    )(x, g.reshape(1, d), w)


def _matmul_norm_res_kernel(y_ref, w_ref, g_ref, x_ref, o_ref, acc_ref):
    k = pl.program_id(1)

    @pl.when(k == 0)
    def _():
        acc_ref[...] = jnp.zeros_like(acc_ref)

    acc_ref[...] += _dot(y_ref[...], w_ref[...])

    @pl.when(k == pl.num_programs(1) - 1)
    def _():
        o_ref[...] = x_ref[...] + _rms(acc_ref[...], g_ref[...])


def matmul_norm_res(y, w, g, x, tm, tk):
    t, kdim = y.shape
    d = w.shape[1]
    tm = min(tm, t)
    tk = min(tk, kdim)
    return pl.pallas_call(
        _matmul_norm_res_kernel,
        grid=(t // tm, kdim // tk),
        in_specs=[
            pl.BlockSpec((tm, tk), lambda i, k: (i, k)),
            pl.BlockSpec((tk, d), lambda i, k: (k, 0)),
            pl.BlockSpec((1, d), lambda i, k: (0, 0)),
            pl.BlockSpec((tm, d), lambda i, k: (i, 0)),
        ],
        out_specs=pl.BlockSpec((tm, d), lambda i, k: (i, 0)),
        out_shape=jax.ShapeDtypeStruct((t, d), F32),
        scratch_shapes=[pltpu.VMEM((tm, d), F32)],
        compiler_params=_params(("parallel", "arbitrary")),
        name="matmul_norm_res",
    )(y, w, g.reshape(1, d), x)


def _mem_kv_kernel(m_ref, g_ref, w_ref, o_ref):
    o_ref[...] = _dot(_rms(m_ref[...], g_ref[...]).astype(BF16), w_ref[...]).astype(o_ref.dtype)


def mem_kv(mem2d, g, w):
    t, d = mem2d.shape
    n = w.shape[1]
    tm = min(256, t)
    return pl.pallas_call(
        _mem_kv_kernel,
        grid=(t // tm,),
        in_specs=[
            pl.BlockSpec((tm, d), lambda i: (i, 0)),
            pl.BlockSpec((1, d), lambda i: (0, 0)),
            pl.BlockSpec((d, n), lambda i: (0, 0)),
        ],
        out_specs=pl.BlockSpec((tm, n), lambda i: (i, 0)),
        out_shape=jax.ShapeDtypeStruct((t, n), BF16),
        compiler_params=_params(("parallel",)),
        name="mem_kv",
    )(mem2d, g.reshape(1, d), w)


def _xattn_kernel(x_ref, g_pre_ref, wq_ref, kv_ref, wo_ref, g_post_ref, o_ref):
    x = x_ref[...]
    h = _rms(x, g_pre_ref[...]).astype(BF16)
    q = _dot(h, wq_ref[...]).astype(BF16)
    kv = kv_ref[...]
    heads = []
    for a in range(XA_H):
        qa = q[:, a * XA_DH:(a + 1) * XA_DH]
        ka = kv[:, a * XA_DH:(a + 1) * XA_DH]
        va = kv[:, (XA_H + a) * XA_DH:(XA_H + a + 1) * XA_DH]
        s = _dot_nt(qa, ka) * (XA_DH ** -0.5)
        s = s - jnp.max(s, axis=-1, keepdims=True)
        e = jnp.exp(s)
        p = e / jnp.sum(e, axis=-1, keepdims=True)
        heads.append(_dot(p.astype(BF16), va).astype(BF16))
    o = jnp.concatenate(heads, axis=-1)
    o_ref[...] = x + _rms(_dot(o, wo_ref[...]), g_post_ref[...])


def xattn(x, g_pre, wq, kv, wo, g_post, seq, tm):
    t, d = x.shape
    tm = min(tm, seq)
    blocks_per_batch = seq // tm
    nq = wq.shape[1]
    return pl.pallas_call(
        _xattn_kernel,
        grid=(t // tm,),
        in_specs=[
            pl.BlockSpec((tm, d), lambda i: (i, 0)),
            pl.BlockSpec((1, d), lambda i: (0, 0)),
            pl.BlockSpec((d, nq), lambda i: (0, 0)),
            pl.BlockSpec((MEM_LEN, 2 * nq), lambda i: (i // blocks_per_batch, 0)),
            pl.BlockSpec((nq, d), lambda i: (0, 0)),
            pl.BlockSpec((1, d), lambda i: (0, 0)),
        ],
        out_specs=pl.BlockSpec((tm, d), lambda i: (i, 0)),
        out_shape=jax.ShapeDtypeStruct((t, d), F32),
        compiler_params=_params(("parallel",)),
        name="xattn",
    )(x, g_pre.reshape(1, d), wq, kv, wo, g_post.reshape(1, d))


def _mlp_kernel(x_ref, g_pre_ref, wu_ref, wd_ref, g_post_ref, o_ref, h_ref, acc_ref):
    j = pl.program_id(1)

    @pl.when(j == 0)
    def _():
        h_ref[...] = _rms(x_ref[...], g_pre_ref[...]).astype(BF16)
        acc_ref[...] = jnp.zeros_like(acc_ref)

    u = jnp.maximum(_dot(h_ref[...], wu_ref[...]), 0.0)
    acc_ref[...] += _dot((u * u).astype(BF16), wd_ref[...])

    @pl.when(j == pl.num_programs(1) - 1)
    def _():
        o_ref[...] = x_ref[...] + _rms(acc_ref[...], g_post_ref[...])


def mlp(x, g_pre, wu, wd, g_post, tm, tf):
    t, d = x.shape
    f = wu.shape[1]
    tm = min(tm, t)
    return pl.pallas_call(
        _mlp_kernel,
        grid=(t // tm, f // tf),
        in_specs=[
            pl.BlockSpec((tm, d), lambda i, j: (i, 0)),
            pl.BlockSpec((1, d), lambda i, j: (0, 0)),
            pl.BlockSpec((d, tf), lambda i, j: (0, j)),
            pl.BlockSpec((tf, d), lambda i, j: (j, 0)),
            pl.BlockSpec((1, d), lambda i, j: (0, 0)),
        ],
        out_specs=pl.BlockSpec((tm, d), lambda i, j: (i, 0)),
        out_shape=jax.ShapeDtypeStruct((t, d), F32),
        scratch_shapes=[pltpu.VMEM((tm, d), BF16), pltpu.VMEM((tm, d), F32)],
        compiler_params=_params(("parallel", "arbitrary")),
        name="mlp",
    )(x, g_pre.reshape(1, d), wu, wd, g_post.reshape(1, d))


def _mlstm_kernel(p_ref, gc_ref, gr_ref, bc_ref, br_ref, hg_ref, y_ref, c_ref, n_ref, m_ref):
    L = p_ref.shape[0]

    @pl.when(pl.program_id(1) == 0)
    def _():
        c_ref[...] = jnp.zeros_like(c_ref)
        n_ref[...] = jnp.zeros_like(n_ref)
        m_ref[...] = jnp.zeros_like(m_ref)

    row = lax.broadcasted_iota(jnp.int32, (L, L), 0)
    col = lax.broadcasted_iota(jnp.int32, (L, L), 1)
    causal = row >= col
    tril = causal.astype(BF16)
    triu = (row <= col).astype(BF16)

    gc = GATE_SOFTCAP * jnp.tanh((gc_ref[...] + bc_ref[...]) / GATE_SOFTCAP)
    lfc = -_softplus(-gc)
    bcum_c_all = _cumsum_rows(tril, lfc)
    gr = GATE_SOFTCAP * jnp.tanh((gr_ref[...] + br_ref[...]) / GATE_SOFTCAP)
    lfr = -_softplus(-gr)
    bcum_r_all = _cumsum_cols(lfr, triu)

    qo, ko, vo, oo = 0, ML_H * ML_DQK, 2 * ML_H * ML_DQK, 2 * ML_H * ML_DQK + ML_H * ML_DV
    for h in range(ML_H):
        q = p_ref[:, qo + h * ML_DQK: qo + (h + 1) * ML_DQK] * (ML_DQK ** -0.5)
        k = p_ref[:, ko + h * ML_DQK: ko + (h + 1) * ML_DQK]
        v = p_ref[:, vo + h * ML_DV: vo + (h + 1) * ML_DV]
        og = p_ref[:, oo + h * ML_DV: oo + (h + 1) * ML_DV].astype(F32)

        ig_c = gc[:, h:h + 1]
        ig_r = gr[h:h + 1, :]
        bcum_c = bcum_c_all[:, ML_H + h: ML_H + h + 1]
        bcum_r = bcum_r_all[ML_H + h: ML_H + h + 1, :]
        b_end = bcum_r[:, L - 1:L]
        m_prev = m_ref[h]
        c_prev = c_ref[h]
        n_prev = n_ref[h]

        dmat = jnp.where(causal, bcum_c - bcum_r + ig_r, NEG_BIG)
        dmx = jnp.max(dmat, axis=-1, keepdims=True)
        inter = bcum_c + m_prev
        mt = jnp.maximum(inter, dmx)
        a_inter = jnp.exp(inter - mt)
        pm = _dot_nt(q, k) * jnp.exp(dmat - mt)
        num = a_inter * _dot(q, c_prev.astype(BF16)) + _dot(pm.astype(BF16), v)
        den = (a_inter * jnp.sum(q.astype(F32) * n_prev, axis=-1, keepdims=True)
               + jnp.sum(pm, axis=-1, keepdims=True))
        out = num * (1.0 / jnp.maximum(jnp.abs(den), jnp.exp(-mt)))

        w_end_r = b_end - bcum_r + ig_r
        m_new = jnp.maximum(b_end + m_prev, jnp.max(w_end_r, axis=-1, keepdims=True))
        a_state = jnp.exp(b_end + m_prev - m_new)
        wk = k.astype(F32) * jnp.exp(b_end - bcum_c + ig_c - m_new)
        c_ref[h] = a_state * c_prev + _dot_tn(wk.astype(BF16), v)
        n_ref[h] = a_state * n_prev + jnp.sum(wk, axis=0, keepdims=True)
        m_ref[h] = m_new

        hs = _rms(out, hg_ref[:, h * ML_DV:(h + 1) * ML_DV])
        y_ref[:, h * ML_DV:(h + 1) * ML_DV] = (hs * _sigmoid(og)).astype(y_ref.dtype)


def mlstm_scan(p, gates_col, gates_row, bias_col, bias_row, head_g, batch, seq):
    t = p.shape[0]
    L = min(ML_CHUNK, seq)
    nc = seq // L
    return pl.pallas_call(
        _mlstm_kernel,
        grid=(batch, nc),
        in_specs=[
            pl.BlockSpec((L, p.shape[1]), lambda b, c: (b * nc + c, 0)),
            pl.BlockSpec((L, LANES), lambda b, c: (b * nc + c, 0)),
            pl.BlockSpec((2 * ML_H, L), lambda b, c: (0, b * nc + c)),
            pl.BlockSpec((1, LANES), lambda b, c: (0, 0)),
            pl.BlockSpec((2 * ML_H, 1), lambda b, c: (0, 0)),
            pl.BlockSpec((1, ML_H * ML_DV), lambda b, c: (0, 0)),
        ],
        out_specs=pl.BlockSpec((L, ML_H * ML_DV), lambda b, c: (b * nc + c, 0)),
        out_shape=jax.ShapeDtypeStruct((t, ML_H * ML_DV), BF16),
        scratch_shapes=[
            pltpu.VMEM((ML_H, ML_DQK, ML_DV), F32),
            pltpu.VMEM((ML_H, 1, ML_DQK), F32),
            pltpu.VMEM((ML_H, 1, 1), F32),
        ],
        compiler_params=_params(("parallel", "arbitrary")),
        name="mlstm_scan",
    )(p, gates_col, gates_row, bias_col, bias_row, head_g)


def _gdn_conv_kernel(x_ref, halo_ref, w_ref, o_ref, *, blocks_per_batch, n_q_blocks, n_qk_blocks):
    i = pl.program_id(0)
    j = pl.program_id(1)
    tm, tc = x_ref.shape
    x = x_ref[...].astype(F32)
    halo = halo_ref[...].astype(F32)
    halo = jnp.where(i % blocks_per_batch == 0, jnp.zeros_like(halo), halo)
    xp = jnp.concatenate([halo, x], axis=0)
    w = w_ref[...]
    acc = xp[SUBLANES:, :] * w[CONV_K - 1:CONV_K, :]
    for s in range(1, CONV_K):
        acc = acc + xp[SUBLANES - s: SUBLANES - s + tm, :] * w[CONV_K - 1 - s: CONV_K - s, :]
    y = acc * _sigmoid(acc)

    @pl.when(j < n_qk_blocks)
    def _():
        scale = jnp.where(j < n_q_blocks, GD ** -0.5, 1.0)
        for c in range(tc // GD):
            blk = y[:, c * GD:(c + 1) * GD]
            ss = jnp.sum(blk * blk, axis=-1, keepdims=True)
            o_ref[:, c * GD:(c + 1) * GD] = (blk * (lax.rsqrt(ss + EPS) * scale)).astype(o_ref.dtype)

    @pl.when(j >= n_qk_blocks)
    def _():
        o_ref[...] = y.astype(o_ref.dtype)


def gdn_conv(p, conv_w, seq, tm, tc):
    t = p.shape[0]
    nqkv = conv_w.shape[1]
    tm = min(tm, seq)
    blocks_per_batch = seq // tm
    halo_blocks = tm // SUBLANES
    kern = functools.partial(
        _gdn_conv_kernel, blocks_per_batch=blocks_per_batch,
        n_q_blocks=GK_H * GD // tc, n_qk_blocks=2 * GK_H * GD // tc)
    return pl.pallas_call(
        kern,
        grid=(t // tm, nqkv // tc),
        in_specs=[
            pl.BlockSpec((tm, tc), lambda i, j: (i, j)),
            pl.BlockSpec((SUBLANES, tc), lambda i, j: (jnp.maximum(i * halo_blocks - 1, 0), j)),
            pl.BlockSpec((CONV_K, tc), lambda i, j: (0, j)),
        ],
        out_specs=pl.BlockSpec((tm, tc), lambda i, j: (i, j)),
        out_shape=jax.ShapeDtypeStruct((t, nqkv), BF16),
        compiler_params=_params(("parallel", "parallel")),
        name="gdn_conv",
    )(p, p, conv_w)


def _bmm(a, b):
    return jnp.einsum("pmk,pkn->pmn", a, b, preferred_element_type=F32)


def _bmm_nt(a, b):
    return jnp.einsum("pmk,pnk->pmn", a, b, preferred_element_type=F32)


def _bmm_tn(a, b):
    return jnp.einsum("pkm,pkn->pmn", a, b, preferred_element_type=F32)


def _unit_lower_inverse(a, eye, same_block):
    L = a.shape[-1]
    d = jnp.where(same_block[GDN_INV_BASE], a, 0.0)
    pw = (-d).astype(BF16)
    t = eye - d
    for _ in range(GDN_INV_BASE.bit_length() - 2):
        pw = _bmm(pw, pw).astype(BF16)
        t = t + _bmm(t.astype(BF16), pw)
    size = GDN_INV_BASE
    while size < L:
        off = jnp.where(same_block[2 * size] & jnp.logical_not(same_block[size]), a, 0.0)
        x = _bmm(off.astype(BF16), t.astype(BF16))
        t = t - _bmm(t.astype(BF16), x.astype(BF16))
        size *= 2
    return t


def _gdn_kernel(q_ref, k_ref, v_ref, z_ref, gc_ref, gr_ref, pc_ref, pr_ref, ng_ref, y_ref, s_ref):
    G = GDN_HEADS_PER_STEP
    L = GDN_CHUNK
    rows = q_ref.shape[0]
    nch = rows // L

    @pl.when(pl.program_id(2) == 0)
    def _():
        s_ref[...] = jnp.zeros_like(s_ref)

    row = lax.broadcasted_iota(jnp.int32, (L, L), 0)
    col = lax.broadcasted_iota(jnp.int32, (L, L), 1)
    causal = row >= col
    strict = row > col
    eye = (row == col).astype(F32)
    triu = (row <= col).astype(BF16)
    same_block = {}
    size = GDN_INV_BASE
    while size <= L:
        shift = size.bit_length() - 1
        same_block[size] = (row >> shift) == (col >> shift)
        size *= 2
    brow = lax.broadcasted_iota(jnp.int32, (rows, rows), 0)
    bcol = lax.broadcasted_iota(jnp.int32, (rows, rows), 1)
    lshift = L.bit_length() - 1
    chunk_tril = (((brow >> lshift) == (bcol >> lshift)) & (brow >= bcol)).astype(BF16)

    gcol = gc_ref[...]
    pc = pc_ref[...]
    beta_all = _sigmoid(gcol)
    g_c = -jnp.exp(pc[0:1, :]) * _softplus(gcol + pc[1:2, :])
    gam_c_all = _cumsum_rows(chunk_tril, g_c)
    pr = pr_ref[...]
    g_r = -jnp.exp(pr[:, 0:1]) * _softplus(gr_ref[...] + pr[:, 1:2])
    gam_r_all = _cumsum_cols(g_r.reshape(nch * 2 * G, L), triu).reshape(nch, 2 * G, L)

    order = [(cc, j) for cc in range(nch) for j in range(G)]
    korder = [(cc, kh) for cc in range(nch) for kh in range(G // 2)]

    def rows_of(cc):
        return slice(cc * L, (cc + 1) * L)

    def lanes_of(h):
        return slice(h * GD, (h + 1) * GD)

    beta = jnp.stack([beta_all[rows_of(cc), j:j + 1] for cc, j in order])
    gam_c = jnp.stack([gam_c_all[rows_of(cc), G + j:G + j + 1] for cc, j in order])
    gam_r = jnp.stack([gam_r_all[cc, G + j:G + j + 1, :] for cc, j in order])
    gam_last = gam_r[:, :, L - 1:L]

    k_k = jnp.stack([k_ref[rows_of(cc), lanes_of(kh)] for cc, kh in korder])
    q_k = jnp.stack([q_ref[rows_of(cc), lanes_of(kh)] for cc, kh in korder])
    kk_k = _bmm_nt(k_k, k_k)
    qk_k = _bmm_nt(q_k, k_k)

    def per_value_head(x):
        return jnp.stack([x[cc * (G // 2) + j // 2] for cc, j in order])

    kk = per_value_head(kk_k)
    qk = per_value_head(qk_k)
    kf = per_value_head(k_k).astype(F32)
    qf = per_value_head(q_k).astype(F32)
    v = jnp.stack([v_ref[rows_of(cc), lanes_of(j)] for cc, j in order]).astype(F32)

    e = jnp.exp(jnp.where(causal, gam_c - gam_r, NEG_BIG))
    a = beta * kk * jnp.where(strict, e, 0.0)
    aqk = (qk * e).astype(BF16)
    tinv = _unit_lower_inverse(a, eye, same_block)

    egam = jnp.exp(gam_c)
    rhs = jnp.concatenate([v * beta, kf * (beta * egam)], axis=-1).astype(BF16)
    uw = _bmm(tinv.astype(BF16), rhs)
    u = uw[:, :, :GD]
    lhs = jnp.concatenate([uw[:, :, GD:].astype(BF16), (qf * egam).astype(BF16)], axis=1)
    kd = (kf * jnp.exp(gam_last - gam_c)).astype(BF16)
    dl = jnp.exp(gam_last)

    s = s_ref[...]
    ng = ng_ref[...]
    for cc in range(nch):
        sl = slice(cc * G, (cc + 1) * G)
        ws_qs = _bmm(lhs[sl], s.astype(BF16))
        vn = (u[sl] - ws_qs[:, :L, :]).astype(BF16)
        o = ws_qs[:, L:, :] + _bmm(aqk[sl], vn)
        s = dl[sl] * s + _bmm_tn(kd[sl], vn)
        hs = _rms(o, ng)
        for j in range(G):
            z = z_ref[rows_of(cc), lanes_of(j)].astype(F32)
            y_ref[rows_of(cc), lanes_of(j)] = (hs[j] * (z * _sigmoid(z))).astype(y_ref.dtype)
    s_ref[...] = s


def gdn_scan(qkv, z_src, z_col0, gates_col, gates_row, par_col, par_row, norm_g, batch, seq):
    t = qkv.shape[0]
    G = GDN_HEADS_PER_STEP
    L = GDN_CHUNK
    rows = min(GDN_ROWS_PER_STEP, seq)
    nrb = seq // rows
    nch = rows // L
    ng = GV_H // G
    qw = (G // 2) * GD
    vw = G * GD
    k_blk0 = GK_H * GD // qw
    v_blk0 = 2 * GK_H * GD // vw
    z_blk0 = z_col0 // vw
    return pl.pallas_call(
        _gdn_kernel,
        grid=(batch, ng, nrb),
        in_specs=[
            pl.BlockSpec((rows, qw), lambda b, g, c: (b * nrb + c, g)),
            pl.BlockSpec((rows, qw), lambda b, g, c: (b * nrb + c, k_blk0 + g)),
            pl.BlockSpec((rows, vw), lambda b, g, c: (b * nrb + c, v_blk0 + g)),
            pl.BlockSpec((rows, vw), lambda b, g, c: (b * nrb + c, z_blk0 + g)),
            pl.BlockSpec((None, rows, 2 * G), lambda b, g, c: (g, b * nrb + c, 0)),
            pl.BlockSpec((None, nch, 2 * G, L), lambda b, g, c: (g, b * nrb + c, 0, 0)),
            pl.BlockSpec((None, 2, 2 * G), lambda b, g, c: (g, 0, 0)),
            pl.BlockSpec((None, 2 * G, 2), lambda b, g, c: (g, 0, 0)),
            pl.BlockSpec((1, GD), lambda b, g, c: (0, 0)),
        ],
        out_specs=pl.BlockSpec((rows, vw), lambda b, g, c: (b * nrb + c, g)),
        out_shape=jax.ShapeDtypeStruct((t, GV_H * GD), BF16),
        scratch_shapes=[pltpu.VMEM((G, GD, GD), F32)],
        compiler_params=_params(("parallel", "parallel", "arbitrary")),
        name="gdn_scan",
    )(qkv, qkv, qkv, z_src, gates_col, gates_row, par_col, par_row, norm_g.reshape(1, GD))


def _pad_cols(w, n):
    return jnp.pad(w, ((0, 0), (0, n - w.shape[1])))


def mlstm_layer(x, g_pre, g_post, w_in, b_gates, head_g, w_out, batch, seq):
    n_main = 2 * ML_H * ML_DQK + 2 * ML_H * ML_DV
    w_main = w_in[:, :n_main].astype(BF16)
    w_gate = _pad_cols(w_in[:, n_main:], LANES).astype(BF16)
    p = norm_matmul(x, g_pre, w_main, BF16, 1024, 512)
    gates_col = norm_matmul(x, g_pre, w_gate, F32, 1024, LANES)
    gates_row = gates_col[:, :2 * ML_H].T
    bias_col = _pad_cols(b_gates.reshape(1, -1), LANES)
    bias_row = b_gates.reshape(-1, 1)
    y = mlstm_scan(p, gates_col, gates_row, bias_col, bias_row, head_g.reshape(1, -1), batch, seq)
    return matmul_norm_res(y, w_out.astype(BF16), g_post, x, 512, 1024)


def gdn_layer(x, g_pre, g_post, w_in, conv_w, a_log, dt_bias, norm_g, w_out, batch, seq):
    G = GDN_HEADS_PER_STEP
    ng = GV_H // G
    t = x.shape[0]
    n_qkv = 2 * GK_H * GD + GV_H * GD
    n_main = n_qkv + GV_H * GD
    w_main = w_in[:, :n_main].astype(BF16)
    w_gate = _pad_cols(w_in[:, n_main:], LANES).astype(BF16)
    p = norm_matmul(x, g_pre, w_main, BF16, 1024, 512)
    gates = norm_matmul(x, g_pre, w_gate, F32, 1024, LANES)[:, :2 * GV_H]
    gates_col = gates.reshape(t, 2, ng, G).transpose(2, 0, 1, 3).reshape(ng, t, 2 * G)
    gates_row = gates_col.reshape(ng, t // GDN_CHUNK, GDN_CHUNK, 2 * G).transpose(0, 1, 3, 2)
    zeros = jnp.zeros((ng, G), F32)
    a_grp = jnp.concatenate([zeros, a_log.reshape(ng, G)], axis=-1)
    dt_grp = jnp.concatenate([zeros, dt_bias.reshape(ng, G)], axis=-1)
    par_col = jnp.stack([a_grp, dt_grp], axis=1)
    par_row = par_col.transpose(0, 2, 1)
    qkv = gdn_conv(p, conv_w, seq, 512, 512)
    y = gdn_scan(qkv, p, n_qkv, gates_col, gates_row, par_col, par_row, norm_g, batch, seq)
    return matmul_norm_res(y, w_out.astype(BF16), g_post, x, 512, 1024)


def kernel(x, mem, norm_g, mem_norm_g, w_mem_kv, w_xq, w_xo, w_up, w_down, mlstm_w_in, mlstm_b_gates, mlstm_head_g, mlstm_w_out, gdn_w_in, gdn_conv_w, gdn_a_log, gdn_dt_bias, gdn_norm_g, gdn_w_out):
    batch, seq, d = x.shape
    depth = norm_g.shape[0]
    kv = mem_kv(mem.reshape(batch * MEM_LEN, d), mem_norm_g, w_mem_kv.astype(BF16))
    xt = x.reshape(batch * seq, d)
    for i in range(depth):
        g = norm_g[i]
        j = i // 2
        if i % 2 == 0:
            xt = mlstm_layer(xt, g[0], g[1], mlstm_w_in[j], mlstm_b_gates[j], mlstm_head_g[j],
                             mlstm_w_out[j], batch, seq)
        else:
            xt = gdn_layer(xt, g[0], g[1], gdn_w_in[j], gdn_conv_w[j], gdn_a_log[j], gdn_dt_bias[j],
                           gdn_norm_g[j], gdn_w_out[j], batch, seq)
        xt = xattn(xt, g[2], w_xq[i].astype(BF16), kv, w_xo[i].astype(BF16), g[3], seq, 512)
        xt = mlp(xt, g[4], w_up[i].astype(BF16), w_down[i].astype(BF16), g[5], 512, 512)
    return xt.reshape(batch, seq, d)
```

```python
import functools

import jax
import jax.numpy as jnp
from jax import lax
from jax.experimental import pallas as pl
from jax.experimental.pallas import tpu as pltpu

EPS = 1e-6
GATE_SOFTCAP = 15.0
D_MODEL = 2048
ML_H = 4
ML_DV = D_MODEL // ML_H
ML_DQK = ML_DV // 2
ML_CHUNK = 256
GK_H = 16
GV_H = 32
GD = 128
CONV_K = 4
GDN_CHUNK = 64
GDN_HEADS_PER_STEP = 8
GDN_ROWS_PER_STEP = 256
GDN_INV_BASE = 8
MEM_LEN = 256
XA_H = 4
XA_DH = 128

LANES = 128
SUBLANES = 8
NEG_BIG = -1e30
VMEM_LIMIT = 56 * 1024 * 1024

BF16 = jnp.bfloat16
F32 = jnp.float32


def _params(sem):
    return pltpu.CompilerParams(dimension_semantics=sem, vmem_limit_bytes=VMEM_LIMIT)


def _rms(xf, g):
    ms = jnp.mean(xf * xf, axis=-1, keepdims=True)
    return xf * lax.rsqrt(ms + EPS) * g


def _dot(a, b):
    return jnp.dot(a, b, preferred_element_type=F32)


def _dot_nt(a, b):
    return lax.dot_general(a, b, (((1,), (1,)), ((), ())), preferred_element_type=F32)


def _bmm(a, b):
    return jnp.einsum("pmk,pkn->pmn", a, b, preferred_element_type=F32)


def _bmm_nt(a, b):
    return jnp.einsum("pmk,pnk->pmn", a, b, preferred_element_type=F32)


def _bmm_tn(a, b):
    return jnp.einsum("pkm,pkn->pmn", a, b, preferred_element_type=F32)


def _split3(x):
    hi = x.astype(BF16)
    r = x - hi.astype(F32)
    mid = r.astype(BF16)
    lo = (r - mid.astype(F32)).astype(BF16)
    return hi, mid, lo


def _cumsum_rows(tri, x):
    hi, mid, lo = _split3(x)
    return _dot(tri, hi) + _dot(tri, mid) + _dot(tri, lo)


def _cumsum_cols(x, tri):
    hi, mid, lo = _split3(x)
    return _dot(hi, tri) + _dot(mid, tri) + _dot(lo, tri)


def _softplus(x):
    return jnp.maximum(x, 0.0) + jnp.log1p(jnp.exp(-jnp.abs(x)))


def _sigmoid(x):
    return 1.0 / (1.0 + jnp.exp(-x))


def _norm_matmul_kernel(x_ref, g_ref, w_ref, o_ref, h_ref):
    @pl.when(pl.program_id(1) == 0)
    def _():
        h_ref[...] = _rms(x_ref[...], g_ref[...]).astype(BF16)

    o_ref[...] = _dot(h_ref[...], w_ref[...]).astype(o_ref.dtype)


def norm_matmul(x, g, w, layer, n, out_dtype, tm, tn):
    t, d = x.shape
    tm = min(tm, t)
    tn = min(tn, n)
    return pl.pallas_call(
        _norm_matmul_kernel,
        grid=(t // tm, n // tn),
        in_specs=[
            pl.BlockSpec((tm, d), lambda i, j: (i, 0)),
            pl.BlockSpec((1, d), lambda i, j: (0, 0)),
            pl.BlockSpec((None, d, tn), lambda i, j: (layer, 0, j)),
        ],
        out_specs=pl.BlockSpec((tm, tn), lambda i, j: (i, j)),
        out_shape=jax.ShapeDtypeStruct((t, n), out_dtype),
        scratch_shapes=[pltpu.VMEM((tm, d), BF16)],
        compiler_params=_params(("parallel", "arbitrary")),
        name="norm_matmul",
    )(x, g.reshape(1, d), w)


def _matmul_norm_res_kernel(y_ref, w_ref, g_ref, x_ref, o_ref):
    k = pl.program_id(1)
    part = _dot(y_ref[...], w_ref[...])

    @pl.when(k == 0)
    def _():
        o_ref[...] = part

    @pl.when(k > 0)
    def _():
        o_ref[...] += part

    @pl.when(k == pl.num_programs(1) - 1)
    def _():
        o_ref[...] = x_ref[...] + _rms(o_ref[...], g_ref[...])


def matmul_norm_res(y, w, layer, g, x, tm, tk):
    t, kdim = y.shape
    d = w.shape[-1]
    tm = min(tm, t)
    tk = min(tk, kdim)
    return pl.pallas_call(
        _matmul_norm_res_kernel,
        grid=(t // tm, kdim // tk),
        in_specs=[
            pl.BlockSpec((tm, tk), lambda i, k: (i, k)),
            pl.BlockSpec((None, tk, d), lambda i, k: (layer, k, 0)),
            pl.BlockSpec((1, d), lambda i, k: (0, 0)),
            pl.BlockSpec((tm, d), lambda i, k: (i, 0)),
        ],
        out_specs=pl.BlockSpec((tm, d), lambda i, k: (i, 0)),
        out_shape=jax.ShapeDtypeStruct((t, d), F32),
        compiler_params=_params(("parallel", "arbitrary")),
        name="matmul_norm_res",
    )(y, w, g.reshape(1, d), x)


def _mem_kv_kernel(m_ref, g_ref, w_ref, o_ref):
    o_ref[...] = _dot(_rms(m_ref[...], g_ref[...]).astype(BF16), w_ref[...]).astype(o_ref.dtype)


def mem_kv(mem2d, g, w):
    t, d = mem2d.shape
    n = w.shape[1]
    tm = min(256, t)
    return pl.pallas_call(
        _mem_kv_kernel,
        grid=(t // tm,),
        in_specs=[
            pl.BlockSpec((tm, d), lambda i: (i, 0)),
            pl.BlockSpec((1, d), lambda i: (0, 0)),
            pl.BlockSpec((d, n), lambda i: (0, 0)),
        ],
        out_specs=pl.BlockSpec((tm, n), lambda i: (i, 0)),
        out_shape=jax.ShapeDtypeStruct((t, n), BF16),
        compiler_params=_params(("parallel",)),
        name="mem_kv",
    )(mem2d, g.reshape(1, d), w)


def _xattn_kernel(x_ref, g_pre_ref, wq_ref, kv_ref, wo_ref, g_post_ref, o_ref):
    x = x_ref[...]
    h = _rms(x, g_pre_ref[...]).astype(BF16)
    q = _dot(h, wq_ref[...]).astype(BF16)
    kv = kv_ref[...]
    heads = []
    for a in range(XA_H):
        qa = q[:, a * XA_DH:(a + 1) * XA_DH]
        ka = kv[:, a * XA_DH:(a + 1) * XA_DH]
        va = kv[:, (XA_H + a) * XA_DH:(XA_H + a + 1) * XA_DH]
        s = _dot_nt(qa, ka) * (XA_DH ** -0.5)
        s = s - jnp.max(s, axis=-1, keepdims=True)
        e = jnp.exp(s)
        p = e / jnp.sum(e, axis=-1, keepdims=True)
        heads.append(_dot(p.astype(BF16), va).astype(BF16))
    o = jnp.concatenate(heads, axis=-1)
    o_ref[...] = x + _rms(_dot(o, wo_ref[...]), g_post_ref[...])


def xattn(x, g_pre, wq, kv, wo, layer, g_post, seq, tm):
    t, d = x.shape
    tm = min(tm, seq)
    blocks_per_batch = seq // tm
    nq = wq.shape[-1]
    return pl.pallas_call(
        _xattn_kernel,
        grid=(t // tm,),
        in_specs=[
            pl.BlockSpec((tm, d), lambda i: (i, 0)),
            pl.BlockSpec((1, d), lambda i: (0, 0)),
            pl.BlockSpec((None, d, nq), lambda i: (layer, 0, 0)),
            pl.BlockSpec((MEM_LEN, 2 * nq), lambda i: (i // blocks_per_batch, 0)),
            pl.BlockSpec((None, nq, d), lambda i: (layer, 0, 0)),
            pl.BlockSpec((1, d), lambda i: (0, 0)),
        ],
        out_specs=pl.BlockSpec((tm, d), lambda i: (i, 0)),
        out_shape=jax.ShapeDtypeStruct((t, d), F32),
        compiler_params=_params(("parallel",)),
        name="xattn",
    )(x, g_pre.reshape(1, d), wq, kv, wo, g_post.reshape(1, d))


def _mlp_kernel(x_ref, g_pre_ref, wu_ref, wd_ref, g_post_ref, o_ref, h_ref):
    j = pl.program_id(1)

    @pl.when(j == 0)
    def _():
        h_ref[...] = _rms(x_ref[...], g_pre_ref[...]).astype(BF16)

    u = jnp.maximum(_dot(h_ref[...], wu_ref[...]), 0.0)
    part = _dot((u * u).astype(BF16), wd_ref[...])

    @pl.when(j == 0)
    def _():
        o_ref[...] = part

    @pl.when(j > 0)
    def _():
        o_ref[...] += part

    @pl.when(j == pl.num_programs(1) - 1)
    def _():
        o_ref[...] = x_ref[...] + _rms(o_ref[...], g_post_ref[...])


def mlp(x, g_pre, wu, wd, layer, g_post, tm, tf):
    t, d = x.shape
    f = wu.shape[-1]
    tm = min(tm, t)
    return pl.pallas_call(
        _mlp_kernel,
        grid=(t // tm, f // tf),
        in_specs=[
            pl.BlockSpec((tm, d), lambda i, j: (i, 0)),
            pl.BlockSpec((1, d), lambda i, j: (0, 0)),
            pl.BlockSpec((None, d, tf), lambda i, j: (layer, 0, j)),
            pl.BlockSpec((None, tf, d), lambda i, j: (layer, j, 0)),
            pl.BlockSpec((1, d), lambda i, j: (0, 0)),
        ],
        out_specs=pl.BlockSpec((tm, d), lambda i, j: (i, 0)),
        out_shape=jax.ShapeDtypeStruct((t, d), F32),
        scratch_shapes=[pltpu.VMEM((tm, d), BF16)],
        compiler_params=_params(("parallel", "arbitrary")),
        name="mlp",
    )(x, g_pre.reshape(1, d), wu, wd, g_post.reshape(1, d))


def _mlstm_kernel(p_ref, gc_ref, gr_ref, bc_ref, br_ref, hg_ref, y_ref, c_ref, n_ref, m_ref):
    L = p_ref.shape[0]
    H = ML_H

    @pl.when(pl.program_id(1) == 0)
    def _():
        c_ref[...] = jnp.zeros_like(c_ref)
        n_ref[...] = jnp.zeros_like(n_ref)
        m_ref[...] = jnp.zeros_like(m_ref)

    row = lax.broadcasted_iota(jnp.int32, (L, L), 0)
    col = lax.broadcasted_iota(jnp.int32, (L, L), 1)
    causal = row >= col
    tril = causal.astype(BF16)
    triu = (row <= col).astype(BF16)

    gc = GATE_SOFTCAP * jnp.tanh((gc_ref[...] + bc_ref[...]) / GATE_SOFTCAP)
    bcum_c_all = _cumsum_rows(tril, -_softplus(-gc))
    gr = GATE_SOFTCAP * jnp.tanh((gr_ref[...] + br_ref[...]) / GATE_SOFTCAP)
    bcum_r_all = _cumsum_cols(-_softplus(-gr), triu)

    qo, ko, vo, oo = 0, H * ML_DQK, 2 * H * ML_DQK, 2 * H * ML_DQK + H * ML_DV
    q = jnp.stack([p_ref[:, qo + h * ML_DQK: qo + (h + 1) * ML_DQK] for h in range(H)]) * (ML_DQK ** -0.5)
    k = jnp.stack([p_ref[:, ko + h * ML_DQK: ko + (h + 1) * ML_DQK] for h in range(H)])
    v = jnp.stack([p_ref[:, vo + h * ML_DV: vo + (h + 1) * ML_DV] for h in range(H)])
    ig_c = jnp.stack([gc[:, h:h + 1] for h in range(H)])
    ig_r = jnp.stack([gr[h:h + 1, :] for h in range(H)])
    bcum_c = jnp.stack([bcum_c_all[:, H + h:H + h + 1] for h in range(H)])
    bcum_r = jnp.stack([bcum_r_all[H + h:H + h + 1, :] for h in range(H)])
    b_end = bcum_r[:, :, L - 1:L]
    m_prev = m_ref[...]
    c_prev = c_ref[...]
    n_prev = n_ref[...]

    dmat = jnp.where(causal, bcum_c - bcum_r + ig_r, NEG_BIG)
    inter = bcum_c + m_prev
    mt = jnp.maximum(inter, jnp.max(dmat, axis=-1, keepdims=True))
    a_inter = jnp.exp(inter - mt)
    pm = _bmm_nt(q, k) * jnp.exp(dmat - mt)
    num = a_inter * _bmm(q, c_prev.astype(BF16)) + _bmm(pm.astype(BF16), v)
    den = (a_inter * jnp.sum(q.astype(F32) * n_prev, axis=-1, keepdims=True)
           + jnp.sum(pm, axis=-1, keepdims=True))
    out = num * (1.0 / jnp.maximum(jnp.abs(den), jnp.exp(-mt)))

    m_new = jnp.maximum(b_end + m_prev, jnp.max(b_end - bcum_r + ig_r, axis=-1, keepdims=True))
    a_state = jnp.exp(b_end + m_prev - m_new)
    wk = k.astype(F32) * jnp.exp(b_end - bcum_c + ig_c - m_new)
    c_ref[...] = a_state * c_prev + _bmm_tn(wk.astype(BF16), v)
    n_ref[...] = a_state * n_prev + jnp.sum(wk, axis=1, keepdims=True)
    m_ref[...] = m_new

    for h in range(H):
        hs = _rms(out[h], hg_ref[:, h * ML_DV:(h + 1) * ML_DV])
        og = p_ref[:, oo + h * ML_DV: oo + (h + 1) * ML_DV].astype(F32)
        y_ref[:, h * ML_DV:(h + 1) * ML_DV] = (hs * _sigmoid(og)).astype(y_ref.dtype)


def mlstm_scan(p, gates_col, gates_row, bias_col, bias_row, head_g, batch, seq):
    t = p.shape[0]
    L = min(ML_CHUNK, seq)
    nc = seq // L
    return pl.pallas_call(
        _mlstm_kernel,
        grid=(batch, nc),
        in_specs=[
            pl.BlockSpec((L, p.shape[1]), lambda b, c: (b * nc + c, 0)),
            pl.BlockSpec((L, LANES), lambda b, c: (b * nc + c, 0)),
            pl.BlockSpec((2 * ML_H, L), lambda b, c: (0, b * nc + c)),
            pl.BlockSpec((1, LANES), lambda b, c: (0, 0)),
            pl.BlockSpec((2 * ML_H, 1), lambda b, c: (0, 0)),
            pl.BlockSpec((1, ML_H * ML_DV), lambda b, c: (0, 0)),
        ],
        out_specs=pl.BlockSpec((L, ML_H * ML_DV), lambda b, c: (b * nc + c, 0)),
        out_shape=jax.ShapeDtypeStruct((t, ML_H * ML_DV), BF16),
        scratch_shapes=[
            pltpu.VMEM((ML_H, ML_DQK, ML_DV), F32),
            pltpu.VMEM((ML_H, 1, ML_DQK), F32),
            pltpu.VMEM((ML_H, 1, 1), F32),
        ],
        compiler_params=_params(("parallel", "arbitrary")),
        name="mlstm_scan",
    )(p, gates_col, gates_row, bias_col, bias_row, head_g)


def _gdn_inproj_kernel(x_ref, g_ref, w_ref, cw_ref, o_ref, h_ref, acc_ref, tail_ref, *,
                       blocks_per_batch, n_q, n_qk, n_qkv, dot_rows, conv_rows):
    i = pl.program_id(0)
    j = pl.program_id(1)
    tm, tn = o_ref.shape

    @pl.when(j == 0)
    def _():
        h_ref[...] = _rms(x_ref[...], g_ref[...]).astype(BF16)

    def conv_tile(l2):
        first = (i % blocks_per_batch) == 0
        acc_ref[0:SUBLANES, :] = jnp.where(first, 0.0, tail_ref[j])
        cw = cw_ref[...]
        scale = jnp.where(j < n_q, GD ** -0.5, 1.0)
        for k0 in range(0, tm, dot_rows):
            acc_ref[SUBLANES + k0: SUBLANES + k0 + dot_rows, :] = _dot(h_ref[k0:k0 + dot_rows, :], w_ref[...])
            for r0 in range(k0, k0 + dot_rows, conv_rows):
                a = acc_ref[SUBLANES + r0: SUBLANES + r0 + conv_rows, :] * cw[CONV_K - 1:CONV_K, :]
                for s in range(1, CONV_K):
                    a = a + (acc_ref[SUBLANES - s + r0: SUBLANES - s + r0 + conv_rows, :]
                             * cw[CONV_K - 1 - s:CONV_K - s, :])
                y = a * _sigmoid(a)
                if l2:
                    for c in range(tn // GD):
                        blk = y[:, c * GD:(c + 1) * GD]
                        ss = jnp.sum(blk * blk, axis=-1, keepdims=True)
                        o_ref[r0:r0 + conv_rows, c * GD:(c + 1) * GD] = (
                            blk * (lax.rsqrt(ss + EPS) * scale)).astype(o_ref.dtype)
                else:
                    o_ref[r0:r0 + conv_rows, :] = y.astype(o_ref.dtype)
        tail_ref[j] = acc_ref[tm:tm + SUBLANES, :]

    @pl.when(j < n_qk)
    def _():
        conv_tile(True)

    @pl.when((j >= n_qk) & (j < n_qkv))
    def _():
        conv_tile(False)

    @pl.when(j >= n_qkv)
    def _():
        o_ref[...] = _dot(h_ref[...], w_ref[...]).astype(o_ref.dtype)


def gdn_inproj(x, g, w, layer, conv_w, n, seq, tm, tn):
    t, d = x.shape
    n_qkv = conv_w.shape[1]
    tm = min(tm, seq)
    kern = functools.partial(
        _gdn_inproj_kernel, blocks_per_batch=seq // tm, n_q=GK_H * GD // tn, n_qk=2 * GK_H * GD // tn,
        n_qkv=n_qkv // tn, dot_rows=min(256, tm), conv_rows=32)
    last_conv_tile = n_qkv // tn - 1
    return pl.pallas_call(
        kern,
        grid=(t // tm, n // tn),
        in_specs=[
            pl.BlockSpec((tm, d), lambda i, j: (i, 0)),
            pl.BlockSpec((1, d), lambda i, j: (0, 0)),
            pl.BlockSpec((None, d, tn), lambda i, j: (layer, 0, j)),
            pl.BlockSpec((CONV_K, tn), lambda i, j: (0, jnp.minimum(j, last_conv_tile))),
        ],
        out_specs=pl.BlockSpec((tm, tn), lambda i, j: (i, j)),
        out_shape=jax.ShapeDtypeStruct((t, n), BF16),
        scratch_shapes=[
            pltpu.VMEM((tm, d), BF16),
            pltpu.VMEM((tm + SUBLANES, tn), F32),
            pltpu.VMEM((n_qkv // tn, SUBLANES, tn), F32),
        ],
        compiler_params=_params(("arbitrary", "arbitrary")),
        name="gdn_inproj",
    )(x, g.reshape(1, d), w, conv_w)


def _unit_lower_inverse(a, eye, same_block):
    L = a.shape[-1]
    d = jnp.where(same_block[GDN_INV_BASE], a, 0.0)
    pw = (-d).astype(BF16)
    t = eye - d
    for _ in range(GDN_INV_BASE.bit_length() - 2):
        pw = _bmm(pw, pw).astype(BF16)
        t = t + _bmm(t.astype(BF16), pw)
    size = GDN_INV_BASE
    while size < L:
        off = jnp.where(same_block[2 * size] & jnp.logical_not(same_block[size]), a, 0.0)
        x = _bmm(off.astype(BF16), t.astype(BF16))
        t = t - _bmm(t.astype(BF16), x.astype(BF16))
        size *= 2
    return t


def _gdn_kernel(q_ref, k_ref, v_ref, z_ref, gc_ref, gr_ref, pc_ref, pr_ref, ng_ref, y_ref, s_ref):
    G = GDN_HEADS_PER_STEP
    L = GDN_CHUNK
    rows = q_ref.shape[0]
    nch = rows // L

    @pl.when(pl.program_id(2) == 0)
    def _():
        s_ref[...] = jnp.zeros_like(s_ref)

    row = lax.broadcasted_iota(jnp.int32, (L, L), 0)
    col = lax.broadcasted_iota(jnp.int32, (L, L), 1)
    causal = row >= col
    strict = row > col
    eye = (row == col).astype(F32)
    triu = (row <= col).astype(BF16)
    same_block = {}
    size = GDN_INV_BASE
    while size <= L:
        shift = size.bit_length() - 1
        same_block[size] = (row >> shift) == (col >> shift)
        size *= 2
    brow = lax.broadcasted_iota(jnp.int32, (rows, rows), 0)
    bcol = lax.broadcasted_iota(jnp.int32, (rows, rows), 1)
    lshift = L.bit_length() - 1
    chunk_tril = (((brow >> lshift) == (bcol >> lshift)) & (brow >= bcol)).astype(BF16)

    gcol = gc_ref[...]
    pc = pc_ref[...]
    beta_all = _sigmoid(gcol)
    g_c = -jnp.exp(pc[0:1, :]) * _softplus(gcol + pc[1:2, :])
    gam_c_all = _cumsum_rows(chunk_tril, g_c)
    pr = pr_ref[...]
    g_r = -jnp.exp(pr[:, 0:1]) * _softplus(gr_ref[...] + pr[:, 1:2])
    gam_r_all = _cumsum_cols(g_r.reshape(nch * 2 * G, L), triu).reshape(nch, 2 * G, L)

    order = [(cc, j) for cc in range(nch) for j in range(G)]
    korder = [(cc, kh) for cc in range(nch) for kh in range(G // 2)]

    def rows_of(cc):
        return slice(cc * L, (cc + 1) * L)

    def lanes_of(h):
        return slice(h * GD, (h + 1) * GD)

    beta = jnp.stack([beta_all[rows_of(cc), j:j + 1] for cc, j in order])
    gam_c = jnp.stack([gam_c_all[rows_of(cc), G + j:G + j + 1] for cc, j in order])
    gam_r = jnp.stack([gam_r_all[cc, G + j:G + j + 1, :] for cc, j in order])
    gam_last = gam_r[:, :, L - 1:L]

    k_k = jnp.stack([k_ref[rows_of(cc), lanes_of(kh)] for cc, kh in korder])
    q_k = jnp.stack([q_ref[rows_of(cc), lanes_of(kh)] for cc, kh in korder])
    kk_k = _bmm_nt(k_k, k_k)
    qk_k = _bmm_nt(q_k, k_k)

    def per_value_head(x):
        return jnp.stack([x[cc * (G // 2) + j // 2] for cc, j in order])

    kk = per_value_head(kk_k)
    qk = per_value_head(qk_k)
    kf = per_value_head(k_k).astype(F32)
    qf = per_value_head(q_k).astype(F32)
    v = jnp.stack([v_ref[rows_of(cc), lanes_of(j)] for cc, j in order]).astype(F32)

    e = jnp.exp(jnp.where(causal, gam_c - gam_r, NEG_BIG))
    a = beta * kk * jnp.where(strict, e, 0.0)
    aqk = (qk * e).astype(BF16)
    tinv = _unit_lower_inverse(a, eye, same_block)

    egam = jnp.exp(gam_c)
    rhs = jnp.concatenate([v * beta, kf * (beta * egam)], axis=-1).astype(BF16)
    uw = _bmm(tinv.astype(BF16), rhs)
    u = uw[:, :, :GD]
    lhs = jnp.concatenate([uw[:, :, GD:].astype(BF16), (qf * egam).astype(BF16)], axis=1)
    kd = (kf * jnp.exp(gam_last - gam_c)).astype(BF16)
    dl = jnp.exp(gam_last)

    s = s_ref[...]
    ng = ng_ref[...]
    for cc in range(nch):
        sl = slice(cc * G, (cc + 1) * G)
        ws_qs = _bmm(lhs[sl], s.astype(BF16))
        vn = (u[sl] - ws_qs[:, :L, :]).astype(BF16)
        o = ws_qs[:, L:, :] + _bmm(aqk[sl], vn)
        s = dl[sl] * s + _bmm_tn(kd[sl], vn)
        hs = _rms(o, ng)
        for j in range(G):
            z = z_ref[rows_of(cc), lanes_of(j)].astype(F32)
            y_ref[rows_of(cc), lanes_of(j)] = (hs[j] * (z * _sigmoid(z))).astype(y_ref.dtype)
    s_ref[...] = s


def gdn_scan(p, z_col0, gates_col, gates_row, par_col, par_row, norm_g, batch, seq):
    t = p.shape[0]
    G = GDN_HEADS_PER_STEP
    L = GDN_CHUNK
    rows = min(GDN_ROWS_PER_STEP, seq)
    nrb = seq // rows
    nch = rows // L
    ng = GV_H // G
    qw = (G // 2) * GD
    vw = G * GD
    k_blk0 = GK_H * GD // qw
    v_blk0 = 2 * GK_H * GD // vw
    z_blk0 = z_col0 // vw
    return pl.pallas_call(
        _gdn_kernel,
        grid=(batch, ng, nrb),
        in_specs=[
            pl.BlockSpec((rows, qw), lambda b, g, c: (b * nrb + c, g)),
            pl.BlockSpec((rows, qw), lambda b, g, c: (b * nrb + c, k_blk0 + g)),
            pl.BlockSpec((rows, vw), lambda b, g, c: (b * nrb + c, v_blk0 + g)),
            pl.BlockSpec((rows, vw), lambda b, g, c: (b * nrb + c, z_blk0 + g)),
            pl.BlockSpec((None, rows, 2 * G), lambda b, g, c: (g, b * nrb + c, 0)),
            pl.BlockSpec((None, nch, 2 * G, L), lambda b, g, c: (g, b * nrb + c, 0, 0)),
            pl.BlockSpec((None, 2, 2 * G), lambda b, g, c: (g, 0, 0)),
            pl.BlockSpec((None, 2 * G, 2), lambda b, g, c: (g, 0, 0)),
            pl.BlockSpec((1, GD), lambda b, g, c: (0, 0)),
        ],
        out_specs=pl.BlockSpec((rows, vw), lambda b, g, c: (b * nrb + c, g)),
        out_shape=jax.ShapeDtypeStruct((t, GV_H * GD), BF16),
        scratch_shapes=[pltpu.VMEM((G, GD, GD), F32)],
        compiler_params=_params(("parallel", "parallel", "arbitrary")),
        name="gdn_scan",
    )(p, p, p, p, gates_col, gates_row, par_col, par_row, norm_g.reshape(1, GD))


def _gate_weights(w_in_layer, n_main):
    w = w_in_layer[:, n_main:]
    return jnp.pad(w, ((0, 0), (0, LANES - w.shape[1]))).astype(BF16)[None]


def mlstm_layer(x, g_pre, g_post, w_in, layer, b_gates, head_g, w_out, batch, seq):
    n_main = 2 * ML_H * ML_DQK + 2 * ML_H * ML_DV
    p = norm_matmul(x, g_pre, w_in.astype(BF16), layer, n_main, BF16, 1024, 1024)
    gates_col = norm_matmul(x, g_pre, _gate_weights(w_in[layer], n_main), 0, LANES, F32, 1024, LANES)
    gates_row = gates_col[:, :2 * ML_H].T
    bias_col = jnp.pad(b_gates.reshape(1, -1), ((0, 0), (0, LANES - 2 * ML_H)))
    bias_row = b_gates.reshape(-1, 1)
    y = mlstm_scan(p, gates_col, gates_row, bias_col, bias_row, head_g.reshape(1, -1), batch, seq)
    return matmul_norm_res(y, w_out.astype(BF16), layer, g_post, x, 512, 2048)


def gdn_layer(x, g_pre, g_post, w_in, layer, conv_w, a_log, dt_bias, norm_g, w_out, batch, seq):
    G = GDN_HEADS_PER_STEP
    ng = GV_H // G
    t = x.shape[0]
    n_qkv = 2 * GK_H * GD + GV_H * GD
    n_main = n_qkv + GV_H * GD
    p = gdn_inproj(x, g_pre, w_in.astype(BF16), layer, conv_w, n_main, seq, 1024, 1024)
    gates = norm_matmul(x, g_pre, _gate_weights(w_in[layer], n_main), 0, LANES, F32, 1024, LANES)[:, :2 * GV_H]
    gates_col = gates.reshape(t, 2, ng, G).transpose(2, 0, 1, 3).reshape(ng, t, 2 * G)
    gates_row = gates_col.reshape(ng, t // GDN_CHUNK, GDN_CHUNK, 2 * G).transpose(0, 1, 3, 2)
    zeros = jnp.zeros((ng, G), F32)
    a_grp = jnp.concatenate([zeros, a_log.reshape(ng, G)], axis=-1)
    dt_grp = jnp.concatenate([zeros, dt_bias.reshape(ng, G)], axis=-1)
    par_col = jnp.stack([a_grp, dt_grp], axis=1)
    par_row = par_col.transpose(0, 2, 1)
    y = gdn_scan(p, n_qkv, gates_col, gates_row, par_col, par_row, norm_g, batch, seq)
    return matmul_norm_res(y, w_out.astype(BF16), layer, g_post, x, 512, 2048)


def kernel(x, mem, norm_g, mem_norm_g, w_mem_kv, w_xq, w_xo, w_up, w_down, mlstm_w_in, mlstm_b_gates, mlstm_head_g, mlstm_w_out, gdn_w_in, gdn_conv_w, gdn_a_log, gdn_dt_bias, gdn_norm_g, gdn_w_out):
    batch, seq, d = x.shape
    depth = norm_g.shape[0]
    kv = mem_kv(mem.reshape(batch * MEM_LEN, d), mem_norm_g, w_mem_kv.astype(BF16))
    wq, wo = w_xq.astype(BF16), w_xo.astype(BF16)
    wu, wd = w_up.astype(BF16), w_down.astype(BF16)
    xt = x.reshape(batch * seq, d)
    for i in range(depth):
        g = norm_g[i]
        j = i // 2
        if i % 2 == 0:
            xt = mlstm_layer(xt, g[0], g[1], mlstm_w_in, j, mlstm_b_gates[j], mlstm_head_g[j],
                             mlstm_w_out, batch, seq)
        else:
            xt = gdn_layer(xt, g[0], g[1], gdn_w_in, j, gdn_conv_w[j], gdn_a_log[j], gdn_dt_bias[j],
                           gdn_norm_g[j], gdn_w_out, batch, seq)
        xt = xattn(xt, g[2], wq, kv, wo, i, g[3], seq, 512)
        xt = mlp(xt, g[4], wu, wd, i, g[5], 512, 1024)
    return xt.reshape(batch, seq, d)
```

```python
import functools

import jax
import jax.numpy as jnp
from jax import lax
from jax.experimental import pallas as pl
from jax.experimental.pallas import tpu as pltpu

EPS = 1e-6
GATE_SOFTCAP = 15.0
D_MODEL = 2048
ML_H = 4
ML_DV = D_MODEL // ML_H
ML_DQK = ML_DV // 2
ML_CHUNK = 256
GK_H = 16
GV_H = 32
GD = 128
CONV_K = 4
GDN_CHUNK = 64
GDN_HEADS_PER_STEP = 16
GDN_ROWS_PER_STEP = 256
GDN_INV_BASE = 8
MEM_LEN = 256
XA_H = 4
XA_DH = 128

LANES = 128
SUBLANES = 8
NEG_BIG = -1e30
VMEM_LIMIT = 56 * 1024 * 1024

BF16 = jnp.bfloat16
F32 = jnp.float32


def _params(sem):
    return pltpu.CompilerParams(dimension_semantics=sem, vmem_limit_bytes=VMEM_LIMIT)


def _rms(xf, g):
    ms = jnp.mean(xf * xf, axis=-1, keepdims=True)
    return xf * lax.rsqrt(ms + EPS) * g


def _dot(a, b):
    return jnp.dot(a, b, preferred_element_type=F32)


def _dot_nt(a, b):
    return lax.dot_general(a, b, (((1,), (1,)), ((), ())), preferred_element_type=F32)


def _bmm(a, b):
    return jnp.einsum("pmk,pkn->pmn", a, b, preferred_element_type=F32)


def _bmm_nt(a, b):
    return jnp.einsum("pmk,pnk->pmn", a, b, preferred_element_type=F32)


def _bmm_tn(a, b):
    return jnp.einsum("pkm,pkn->pmn", a, b, preferred_element_type=F32)


def _split3(x):
    hi = x.astype(BF16)
    r = x - hi.astype(F32)
    mid = r.astype(BF16)
    lo = (r - mid.astype(F32)).astype(BF16)
    return hi, mid, lo


def _cumsum_rows(tri, x):
    hi, mid, lo = _split3(x)
    return _dot(tri, hi) + _dot(tri, mid) + _dot(tri, lo)


def _cumsum_cols(x, tri):
    hi, mid, lo = _split3(x)
    return _dot(hi, tri) + _dot(mid, tri) + _dot(lo, tri)


def _softplus(x):
    return jnp.maximum(x, 0.0) + jnp.log1p(jnp.exp(-jnp.abs(x)))


def _sigmoid(x):
    return 1.0 / (1.0 + jnp.exp(-x))


def _norm_matmul_kernel(x_ref, g_ref, w_ref, o_ref, h_ref):
    @pl.when(pl.program_id(1) == 0)
    def _():
        h_ref[...] = _rms(x_ref[...], g_ref[...]).astype(BF16)

    o_ref[...] = _dot(h_ref[...], w_ref[...]).astype(o_ref.dtype)


def norm_matmul(x, g, w, layer, n, out_dtype, tm, tn):
    t, d = x.shape
    tm = min(tm, t)
    tn = min(tn, n)
    return pl.pallas_call(
        _norm_matmul_kernel,
        grid=(t // tm, n // tn),
        in_specs=[
            pl.BlockSpec((tm, d), lambda i, j: (i, 0)),
            pl.BlockSpec((1, d), lambda i, j: (0, 0)),
            pl.BlockSpec((None, d, tn), lambda i, j: (layer, 0, j)),
        ],
        out_specs=pl.BlockSpec((tm, tn), lambda i, j: (i, j)),
        out_shape=jax.ShapeDtypeStruct((t, n), out_dtype),
        scratch_shapes=[pltpu.VMEM((tm, d), BF16)],
        compiler_params=_params(("parallel", "arbitrary")),
        name="norm_matmul",
    )(x, g.reshape(1, d), w)


def _matmul_norm_res_kernel(y_ref, w_ref, g_ref, x_ref, o_ref, *, k_steps):
    if k_steps == 1:
        o_ref[...] = x_ref[...] + _rms(_dot(y_ref[...], w_ref[...]), g_ref[...])
        return
    k = pl.program_id(1)

    @pl.when(k == 0)
    def _():
        o_ref[...] = jnp.zeros_like(o_ref)

    o_ref[...] += _dot(y_ref[...], w_ref[...])

    @pl.when(k == k_steps - 1)
    def _():
        o_ref[...] = x_ref[...] + _rms(o_ref[...], g_ref[...])


def matmul_norm_res(y, w, layer, g, x, tm, tk):
    t, kdim = y.shape
    d = w.shape[-1]
    tm = min(tm, t)
    tk = min(tk, kdim)
    return pl.pallas_call(
        functools.partial(_matmul_norm_res_kernel, k_steps=kdim // tk),
        grid=(t // tm, kdim // tk),
        in_specs=[
            pl.BlockSpec((tm, tk), lambda i, k: (i, k)),
            pl.BlockSpec((None, tk, d), lambda i, k: (layer, k, 0)),
            pl.BlockSpec((1, d), lambda i, k: (0, 0)),
            pl.BlockSpec((tm, d), lambda i, k: (i, 0)),
        ],
        out_specs=pl.BlockSpec((tm, d), lambda i, k: (i, 0)),
        out_shape=jax.ShapeDtypeStruct((t, d), F32),
        compiler_params=_params(("parallel", "arbitrary")),
        name="matmul_norm_res",
    )(y, w, g.reshape(1, d), x)


def _mem_kv_kernel(m_ref, g_ref, w_ref, o_ref):
    o_ref[...] = _dot(_rms(m_ref[...], g_ref[...]).astype(BF16), w_ref[...]).astype(o_ref.dtype)


def mem_kv(mem2d, g, w):
    t, d = mem2d.shape
    n = w.shape[1]
    tm = min(256, t)
    return pl.pallas_call(
        _mem_kv_kernel,
        grid=(t // tm,),
        in_specs=[
            pl.BlockSpec((tm, d), lambda i: (i, 0)),
            pl.BlockSpec((1, d), lambda i: (0, 0)),
            pl.BlockSpec((d, n), lambda i: (0, 0)),
        ],
        out_specs=pl.BlockSpec((tm, n), lambda i: (i, 0)),
        out_shape=jax.ShapeDtypeStruct((t, n), BF16),
        compiler_params=_params(("parallel",)),
        name="mem_kv",
    )(mem2d, g.reshape(1, d), w)


def _xattn_kernel(x_ref, g_pre_ref, wq_ref, kv_ref, wo_ref, g_post_ref, o_ref):
    x = x_ref[...]
    h = _rms(x, g_pre_ref[...]).astype(BF16)
    q = _dot(h, wq_ref[...]).astype(BF16)
    kv = kv_ref[...]
    heads = []
    for a in range(XA_H):
        qa = q[:, a * XA_DH:(a + 1) * XA_DH]
        ka = kv[:, a * XA_DH:(a + 1) * XA_DH]
        va = kv[:, (XA_H + a) * XA_DH:(XA_H + a + 1) * XA_DH]
        s = _dot_nt(qa, ka) * (XA_DH ** -0.5)
        s = s - jnp.max(s, axis=-1, keepdims=True)
        e = jnp.exp(s)
        p = e / jnp.sum(e, axis=-1, keepdims=True)
        heads.append(_dot(p.astype(BF16), va).astype(BF16))
    o = jnp.concatenate(heads, axis=-1)
    o_ref[...] = x + _rms(_dot(o, wo_ref[...]), g_post_ref[...])


def xattn(x, g_pre, wq, kv, wo, layer, g_post, seq, tm):
    t, d = x.shape
    tm = min(tm, seq)
    blocks_per_batch = seq // tm
    nq = wq.shape[-1]
    return pl.pallas_call(
        _xattn_kernel,
        grid=(t // tm,),
        in_specs=[
            pl.BlockSpec((tm, d), lambda i: (i, 0)),
            pl.BlockSpec((1, d), lambda i: (0, 0)),
            pl.BlockSpec((None, d, nq), lambda i: (layer, 0, 0)),
            pl.BlockSpec((MEM_LEN, 2 * nq), lambda i: (i // blocks_per_batch, 0)),
            pl.BlockSpec((None, nq, d), lambda i: (layer, 0, 0)),
            pl.BlockSpec((1, d), lambda i: (0, 0)),
        ],
        out_specs=pl.BlockSpec((tm, d), lambda i: (i, 0)),
        out_shape=jax.ShapeDtypeStruct((t, d), F32),
        compiler_params=_params(("parallel",)),
        name="xattn",
    )(x, g_pre.reshape(1, d), wq, kv, wo, g_post.reshape(1, d))


def _mlp_kernel(x_ref, g_pre_ref, wu_ref, wd_ref, g_post_ref, o_ref, h_ref):
    j = pl.program_id(1)

    @pl.when(j == 0)
    def _():
        h_ref[...] = _rms(x_ref[...], g_pre_ref[...]).astype(BF16)
        o_ref[...] = jnp.zeros_like(o_ref)

    u = jnp.maximum(_dot(h_ref[...], wu_ref[...]), 0.0)
    o_ref[...] += _dot((u * u).astype(BF16), wd_ref[...])

    @pl.when(j == pl.num_programs(1) - 1)
    def _():
        o_ref[...] = x_ref[...] + _rms(o_ref[...], g_post_ref[...])


def mlp(x, g_pre, wu, wd, layer, g_post, tm, tf):
    t, d = x.shape
    f = wu.shape[-1]
    tm = min(tm, t)
    return pl.pallas_call(
        _mlp_kernel,
        grid=(t // tm, f // tf),
        in_specs=[
            pl.BlockSpec((tm, d), lambda i, j: (i, 0)),
            pl.BlockSpec((1, d), lambda i, j: (0, 0)),
            pl.BlockSpec((None, d, tf), lambda i, j: (layer, 0, j)),
            pl.BlockSpec((None, tf, d), lambda i, j: (layer, j, 0)),
            pl.BlockSpec((1, d), lambda i, j: (0, 0)),
        ],
        out_specs=pl.BlockSpec((tm, d), lambda i, j: (i, 0)),
        out_shape=jax.ShapeDtypeStruct((t, d), F32),
        scratch_shapes=[pltpu.VMEM((tm, d), BF16)],
        compiler_params=_params(("parallel", "arbitrary")),
        name="mlp",
    )(x, g_pre.reshape(1, d), wu, wd, g_post.reshape(1, d))


def _mlstm_kernel(p_ref, gc_ref, gr_ref, bc_ref, br_ref, hg_ref, y_ref, c_ref, n_ref, m_ref):
    L = p_ref.shape[0]
    H = ML_H

    @pl.when(pl.program_id(1) == 0)
    def _():
        c_ref[...] = jnp.zeros_like(c_ref)
        n_ref[...] = jnp.zeros_like(n_ref)
        m_ref[...] = jnp.zeros_like(m_ref)

    row = lax.broadcasted_iota(jnp.int32, (L, L), 0)
    col = lax.broadcasted_iota(jnp.int32, (L, L), 1)
    causal = row >= col
    tril = causal.astype(BF16)
    triu = (row <= col).astype(BF16)

    gc = GATE_SOFTCAP * jnp.tanh((gc_ref[...] + bc_ref[...]) / GATE_SOFTCAP)
    bcum_c_all = _cumsum_rows(tril, -_softplus(-gc))
    gr = GATE_SOFTCAP * jnp.tanh((gr_ref[...] + br_ref[...]) / GATE_SOFTCAP)
    bcum_r_all = _cumsum_cols(-_softplus(-gr), triu)

    qo, ko, vo, oo = 0, H * ML_DQK, 2 * H * ML_DQK, 2 * H * ML_DQK + H * ML_DV
    q = jnp.stack([p_ref[:, qo + h * ML_DQK: qo + (h + 1) * ML_DQK] for h in range(H)]) * (ML_DQK ** -0.5)
    k = jnp.stack([p_ref[:, ko + h * ML_DQK: ko + (h + 1) * ML_DQK] for h in range(H)])
    v = jnp.stack([p_ref[:, vo + h * ML_DV: vo + (h + 1) * ML_DV] for h in range(H)])
    ig_c = jnp.stack([gc[:, h:h + 1] for h in range(H)])
    ig_r = jnp.stack([gr[h:h + 1, :] for h in range(H)])
    bcum_c = jnp.stack([bcum_c_all[:, H + h:H + h + 1] for h in range(H)])
    bcum_r = jnp.stack([bcum_r_all[H + h:H + h + 1, :] for h in range(H)])
    b_end = bcum_r[:, :, L - 1:L]
    m_prev = m_ref[...]
    c_prev = c_ref[...]
    n_prev = n_ref[...]

    dmat = jnp.where(causal, bcum_c - bcum_r + ig_r, NEG_BIG)
    inter = bcum_c + m_prev
    mt = jnp.maximum(inter, jnp.max(dmat, axis=-1, keepdims=True))
    a_inter = jnp.exp(inter - mt)
    pm = _bmm_nt(q, k) * jnp.exp(dmat - mt)
    num = a_inter * _bmm(q, c_prev.astype(BF16)) + _bmm(pm.astype(BF16), v)
    den = (a_inter * jnp.sum(q.astype(F32) * n_prev, axis=-1, keepdims=True)
           + jnp.sum(pm, axis=-1, keepdims=True))
    out = num * (1.0 / jnp.maximum(jnp.abs(den), jnp.exp(-mt)))

    m_new = jnp.maximum(b_end + m_prev, jnp.max(b_end - bcum_r + ig_r, axis=-1, keepdims=True))
    a_state = jnp.exp(b_end + m_prev - m_new)
    wk = k.astype(F32) * jnp.exp(b_end - bcum_c + ig_c - m_new)
    c_ref[...] = a_state * c_prev + _bmm_tn(wk.astype(BF16), v)
    n_ref[...] = a_state * n_prev + jnp.sum(wk, axis=1, keepdims=True)
    m_ref[...] = m_new

    for h in range(H):
        hs = _rms(out[h], hg_ref[:, h * ML_DV:(h + 1) * ML_DV])
        og = p_ref[:, oo + h * ML_DV: oo + (h + 1) * ML_DV].astype(F32)
        y_ref[:, h * ML_DV:(h + 1) * ML_DV] = (hs * _sigmoid(og)).astype(y_ref.dtype)


def mlstm_scan(p, gates_col, gates_row, bias_col, bias_row, head_g, batch, seq):
    t = p.shape[0]
    L = min(ML_CHUNK, seq)
    nc = seq // L
    return pl.pallas_call(
        _mlstm_kernel,
        grid=(batch, nc),
        in_specs=[
            pl.BlockSpec((L, p.shape[1]), lambda b, c: (b * nc + c, 0)),
            pl.BlockSpec((L, LANES), lambda b, c: (b * nc + c, 0)),
            pl.BlockSpec((2 * ML_H, L), lambda b, c: (0, b * nc + c)),
            pl.BlockSpec((1, LANES), lambda b, c: (0, 0)),
            pl.BlockSpec((2 * ML_H, 1), lambda b, c: (0, 0)),
            pl.BlockSpec((1, ML_H * ML_DV), lambda b, c: (0, 0)),
        ],
        out_specs=pl.BlockSpec((L, ML_H * ML_DV), lambda b, c: (b * nc + c, 0)),
        out_shape=jax.ShapeDtypeStruct((t, ML_H * ML_DV), BF16),
        scratch_shapes=[
            pltpu.VMEM((ML_H, ML_DQK, ML_DV), F32),
            pltpu.VMEM((ML_H, 1, ML_DQK), F32),
            pltpu.VMEM((ML_H, 1, 1), F32),
        ],
        compiler_params=_params(("parallel", "arbitrary")),
        name="mlstm_scan",
    )(p, gates_col, gates_row, bias_col, bias_row, head_g)


def _gdn_inproj_kernel(x_ref, g_ref, w_ref, cw_ref, o_ref, h_ref, acc_ref, tail_ref, *,
                       blocks_per_batch, n_q, n_qk, n_qkv, dot_rows, conv_rows):
    i = pl.program_id(0)
    j = pl.program_id(1)
    tm, tn = o_ref.shape

    @pl.when(j == 0)
    def _():
        h_ref[...] = _rms(x_ref[...], g_ref[...]).astype(BF16)

    def conv_tile(l2):
        first = (i % blocks_per_batch) == 0
        acc_ref[0:SUBLANES, :] = jnp.where(first, 0.0, tail_ref[j])
        cw = cw_ref[...]
        scale = jnp.where(j < n_q, GD ** -0.5, 1.0)
        for k0 in range(0, tm, dot_rows):
            acc_ref[SUBLANES + k0: SUBLANES + k0 + dot_rows, :] = _dot(h_ref[k0:k0 + dot_rows, :], w_ref[...])
            for r0 in range(k0, k0 + dot_rows, conv_rows):
                a = acc_ref[SUBLANES + r0: SUBLANES + r0 + conv_rows, :] * cw[CONV_K - 1:CONV_K, :]
                for s in range(1, CONV_K):
                    a = a + (acc_ref[SUBLANES - s + r0: SUBLANES - s + r0 + conv_rows, :]
                             * cw[CONV_K - 1 - s:CONV_K - s, :])
                y = a * _sigmoid(a)
                if l2:
                    for c in range(tn // GD):
                        blk = y[:, c * GD:(c + 1) * GD]
                        ss = jnp.sum(blk * blk, axis=-1, keepdims=True)
                        o_ref[r0:r0 + conv_rows, c * GD:(c + 1) * GD] = (
                            blk * (lax.rsqrt(ss + EPS) * scale)).astype(o_ref.dtype)
                else:
                    o_ref[r0:r0 + conv_rows, :] = y.astype(o_ref.dtype)
        tail_ref[j] = acc_ref[tm:tm + SUBLANES, :]

    @pl.when(j < n_qk)
    def _():
        conv_tile(True)

    @pl.when((j >= n_qk) & (j < n_qkv))
    def _():
        conv_tile(False)

    @pl.when(j >= n_qkv)
    def _():
        o_ref[...] = _dot(h_ref[...], w_ref[...]).astype(o_ref.dtype)


def gdn_inproj(x, g, w, layer, conv_w, n, seq, tm, tn):
    t, d = x.shape
    n_qkv = conv_w.shape[1]
    tm = min(tm, seq)
    kern = functools.partial(
        _gdn_inproj_kernel, blocks_per_batch=seq // tm, n_q=GK_H * GD // tn, n_qk=2 * GK_H * GD // tn,
        n_qkv=n_qkv // tn, dot_rows=min(256, tm), conv_rows=32)
    last_conv_tile = n_qkv // tn - 1
    return pl.pallas_call(
        kern,
        grid=(t // tm, n // tn),
        in_specs=[
            pl.BlockSpec((tm, d), lambda i, j: (i, 0)),
            pl.BlockSpec((1, d), lambda i, j: (0, 0)),
            pl.BlockSpec((None, d, tn), lambda i, j: (layer, 0, j)),
            pl.BlockSpec((CONV_K, tn), lambda i, j: (0, jnp.minimum(j, last_conv_tile))),
        ],
        out_specs=pl.BlockSpec((tm, tn), lambda i, j: (i, j)),
        out_shape=jax.ShapeDtypeStruct((t, n), BF16),
        scratch_shapes=[
            pltpu.VMEM((tm, d), BF16),
            pltpu.VMEM((tm + SUBLANES, tn), F32),
            pltpu.VMEM((n_qkv // tn, SUBLANES, tn), F32),
        ],
        compiler_params=_params(("arbitrary", "arbitrary")),
        name="gdn_inproj",
    )(x, g.reshape(1, d), w, conv_w)


def _pair_blockdiag(y):
    L = y.shape[1]
    left = lax.broadcasted_iota(jnp.int32, (L, 2 * L), 1) < L
    zero = jnp.zeros_like(y)
    return jnp.concatenate([jnp.where(left, y, zero), jnp.where(left, zero, y)], axis=1)


def _pair_mm(x, y):
    return _bmm(x, _pair_blockdiag(y))


def _unit_lower_inverse_pairs(a, eye, same_block):
    L = a.shape[1]
    d = jnp.where(same_block[GDN_INV_BASE], a, 0.0)
    pw = (-d).astype(BF16)
    t = eye - d
    for _ in range(GDN_INV_BASE.bit_length() - 2):
        pw = _pair_mm(pw, pw).astype(BF16)
        t = t + _pair_mm(t.astype(BF16), pw)
    size = GDN_INV_BASE
    while size < L:
        off = jnp.where(same_block[2 * size] & jnp.logical_not(same_block[size]), a, 0.0)
        x = _pair_mm(off.astype(BF16), t.astype(BF16))
        t = t - _pair_mm(t.astype(BF16), x.astype(BF16))
        size *= 2
    return t


def _gdn_kernel(q_ref, k_ref, v_ref, z_ref, gc_ref, gr_ref, pc_ref, pr_ref, ng_ref, y_ref, s_ref):
    G = GDN_HEADS_PER_STEP
    L = GDN_CHUNK
    rows = q_ref.shape[0]
    nch = rows // L
    assert 2 * L == GD

    @pl.when(pl.program_id(2) == 0)
    def _():
        s_ref[...] = jnp.zeros_like(s_ref)

    row = lax.broadcasted_iota(jnp.int32, (L, 2 * L), 0)
    lane = lax.broadcasted_iota(jnp.int32, (L, 2 * L), 1)
    col = lane & (L - 1)
    left = lane < L
    causal = row >= col
    strict = row > col
    eye = (row == col).astype(F32)
    same_block = {}
    size = GDN_INV_BASE
    while size <= L:
        shift = size.bit_length() - 1
        same_block[size] = (row >> shift) == (col >> shift)
        size *= 2
    triu = (lax.broadcasted_iota(jnp.int32, (L, L), 0) <= lax.broadcasted_iota(jnp.int32, (L, L), 1)).astype(BF16)
    brow = lax.broadcasted_iota(jnp.int32, (rows, rows), 0)
    bcol = lax.broadcasted_iota(jnp.int32, (rows, rows), 1)
    lshift = L.bit_length() - 1
    chunk_tril = (((brow >> lshift) == (bcol >> lshift)) & (brow >= bcol)).astype(BF16)

    gcol = gc_ref[...]
    pc = pc_ref[...]
    beta_all = _sigmoid(gcol)
    g_c = -jnp.exp(pc[0:1, :]) * _softplus(gcol + pc[1:2, :])
    gam_c_all = _cumsum_rows(chunk_tril, g_c)
    pr = pr_ref[...]
    g_r = -jnp.exp(pr[:, 0:1]) * _softplus(gr_ref[...] + pr[:, 1:2])
    gam_r_all = _cumsum_cols(g_r.reshape(nch * 2 * G, L), triu).reshape(nch, 2 * G, L)

    order = [(cc, j) for cc in range(nch) for j in range(G)]
    pairs = [(cc, kh) for cc in range(nch) for kh in range(G // 2)]

    def rows_of(cc):
        return slice(cc * L, (cc + 1) * L)

    def lanes_of(h):
        return slice(h * GD, (h + 1) * GD)

    def to_pairs(x):
        return jnp.stack([jnp.where(left, x[2 * i], x[2 * i + 1]) for i in range(len(pairs))])

    def per_value_head(x):
        return jnp.stack([x[i // 2] for i in range(len(order))])

    def pad_rows(x):
        zero = jnp.zeros_like(x[0])
        return jnp.stack([jnp.concatenate([x[i], zero] if i % 2 == 0 else [zero, x[i]], axis=0)
                          for i in range(x.shape[0])])

    beta_b = jnp.stack([jnp.broadcast_to(beta_all[rows_of(cc), j:j + 1], (L, GD)) for cc, j in order])
    gam_b = jnp.stack([jnp.broadcast_to(gam_c_all[rows_of(cc), G + j:G + j + 1], (L, GD)) for cc, j in order])
    gam_last = jnp.stack([gam_r_all[cc, G + j:G + j + 1, L - 1:L] for cc, j in order])
    gam_r_p = jnp.stack([
        jnp.concatenate([gam_r_all[cc, G + 2 * kh:G + 2 * kh + 1, :],
                         gam_r_all[cc, G + 2 * kh + 1:G + 2 * kh + 2, :]], axis=-1)
        for cc, kh in pairs])

    k_k = jnp.stack([k_ref[rows_of(cc), lanes_of(kh)] for cc, kh in pairs])
    q_k = jnp.stack([q_ref[rows_of(cc), lanes_of(kh)] for cc, kh in pairs])
    k_twice = jnp.concatenate([k_k, k_k], axis=1)
    kk_p = _bmm_nt(k_k, k_twice)
    qk_p = _bmm_nt(q_k, k_twice)

    e = jnp.exp(jnp.where(causal, to_pairs(gam_b) - gam_r_p, NEG_BIG))
    a = to_pairs(beta_b) * kk_p * jnp.where(strict, e, 0.0)
    aqk = per_value_head((qk_p * e).astype(BF16))
    tinv = per_value_head(_unit_lower_inverse_pairs(a, eye, same_block).astype(BF16))

    kf = per_value_head(k_k).astype(F32)
    qf = per_value_head(q_k).astype(F32)
    v = jnp.stack([v_ref[rows_of(cc), lanes_of(j)] for cc, j in order]).astype(F32)
    egam_b = jnp.exp(gam_b)
    rhs = jnp.concatenate([v * beta_b, kf * beta_b * egam_b], axis=-1).astype(BF16)
    uw = _bmm(tinv, pad_rows(rhs))
    u = uw[:, :, :GD]
    lhs = jnp.concatenate([uw[:, :, GD:].astype(BF16), (qf * egam_b).astype(BF16)], axis=1)
    kd = (kf * jnp.exp(gam_last - gam_b)).astype(BF16)
    dl = jnp.exp(gam_last)

    s = s_ref[...]
    ng = ng_ref[...]
    for cc in range(nch):
        sl = slice(cc * G, (cc + 1) * G)
        ws_qs = _bmm(lhs[sl], s.astype(BF16))
        vn = (u[sl] - ws_qs[:, :L, :]).astype(BF16)
        o = ws_qs[:, L:, :] + _bmm(aqk[sl], pad_rows(vn))
        s = dl[sl] * s + _bmm_tn(kd[sl], vn)
        hs = _rms(o, ng)
        for j in range(G):
            z = z_ref[rows_of(cc), lanes_of(j)].astype(F32)
            y_ref[rows_of(cc), lanes_of(j)] = (hs[j] * (z * _sigmoid(z))).astype(y_ref.dtype)
    s_ref[...] = s


def gdn_scan(p, z_col0, gates_col, gates_row, par_col, par_row, norm_g, batch, seq):
    t = p.shape[0]
    G = GDN_HEADS_PER_STEP
    L = GDN_CHUNK
    rows = min(GDN_ROWS_PER_STEP, seq)
    nrb = seq // rows
    nch = rows // L
    ng = GV_H // G
    qw = (G // 2) * GD
    vw = G * GD
    k_blk0 = GK_H * GD // qw
    v_blk0 = 2 * GK_H * GD // vw
    z_blk0 = z_col0 // vw
    return pl.pallas_call(
        _gdn_kernel,
        grid=(batch, ng, nrb),
        in_specs=[
            pl.BlockSpec((rows, qw), lambda b, g, c: (b * nrb + c, g)),
            pl.BlockSpec((rows, qw), lambda b, g, c: (b * nrb + c, k_blk0 + g)),
            pl.BlockSpec((rows, vw), lambda b, g, c: (b * nrb + c, v_blk0 + g)),
            pl.BlockSpec((rows, vw), lambda b, g, c: (b * nrb + c, z_blk0 + g)),
            pl.BlockSpec((None, rows, 2 * G), lambda b, g, c: (g, b * nrb + c, 0)),
            pl.BlockSpec((None, nch, 2 * G, L), lambda b, g, c: (g, b * nrb + c, 0, 0)),
            pl.BlockSpec((None, 2, 2 * G), lambda b, g, c: (g, 0, 0)),
            pl.BlockSpec((None, 2 * G, 2), lambda b, g, c: (g, 0, 0)),
            pl.BlockSpec((1, GD), lambda b, g, c: (0, 0)),
        ],
        out_specs=pl.BlockSpec((rows, vw), lambda b, g, c: (b * nrb + c, g)),
        out_shape=jax.ShapeDtypeStruct((t, GV_H * GD), BF16),
        scratch_shapes=[pltpu.VMEM((G, GD, GD), F32)],
        compiler_params=_params(("parallel", "parallel", "arbitrary")),
        name="gdn_scan",
    )(p, p, p, p, gates_col, gates_row, par_col, par_row, norm_g.reshape(1, GD))


def _gate_weights(w_in_layer, n_main):
    w = w_in_layer[:, n_main:]
    return jnp.pad(w, ((0, 0), (0, LANES - w.shape[1]))).astype(BF16)[None]


def mlstm_layer(x, g_pre, g_post, w_in, layer, b_gates, head_g, w_out, batch, seq):
    n_main = 2 * ML_H * ML_DQK + 2 * ML_H * ML_DV
    p = norm_matmul(x, g_pre, w_in.astype(BF16), layer, n_main, BF16, 1024, 1024)
    gates_col = norm_matmul(x, g_pre, _gate_weights(w_in[layer], n_main), 0, LANES, F32, 1024, LANES)
    gates_row = gates_col[:, :2 * ML_H].T
    bias_col = jnp.pad(b_gates.reshape(1, -1), ((0, 0), (0, LANES - 2 * ML_H)))
    bias_row = b_gates.reshape(-1, 1)
    y = mlstm_scan(p, gates_col, gates_row, bias_col, bias_row, head_g.reshape(1, -1), batch, seq)
    return matmul_norm_res(y, w_out.astype(BF16), layer, g_post, x, 512, 2048)


def gdn_layer(x, g_pre, g_post, w_in, layer, conv_w, a_log, dt_bias, norm_g, w_out, batch, seq):
    G = GDN_HEADS_PER_STEP
    ng = GV_H // G
    t = x.shape[0]
    n_qkv = 2 * GK_H * GD + GV_H * GD
    n_main = n_qkv + GV_H * GD
    p = gdn_inproj(x, g_pre, w_in.astype(BF16), layer, conv_w, n_main, seq, 1024, 1024)
    gates = norm_matmul(x, g_pre, _gate_weights(w_in[layer], n_main), 0, LANES, F32, 1024, LANES)[:, :2 * GV_H]
    gates_col = gates.reshape(t, 2, ng, G).transpose(2, 0, 1, 3).reshape(ng, t, 2 * G)
    gates_row = gates_col.reshape(ng, t // GDN_CHUNK, GDN_CHUNK, 2 * G).transpose(0, 1, 3, 2)
    zeros = jnp.zeros((ng, G), F32)
    a_grp = jnp.concatenate([zeros, a_log.reshape(ng, G)], axis=-1)
    dt_grp = jnp.concatenate([zeros, dt_bias.reshape(ng, G)], axis=-1)
    par_col = jnp.stack([a_grp, dt_grp], axis=1)
    par_row = par_col.transpose(0, 2, 1)
    y = gdn_scan(p, n_qkv, gates_col, gates_row, par_col, par_row, norm_g, batch, seq)
    return matmul_norm_res(y, w_out.astype(BF16), layer, g_post, x, 512, 2048)


def kernel(x, mem, norm_g, mem_norm_g, w_mem_kv, w_xq, w_xo, w_up, w_down, mlstm_w_in, mlstm_b_gates, mlstm_head_g, mlstm_w_out, gdn_w_in, gdn_conv_w, gdn_a_log, gdn_dt_bias, gdn_norm_g, gdn_w_out):
    batch, seq, d = x.shape
    depth = norm_g.shape[0]
    kv = mem_kv(mem.reshape(batch * MEM_LEN, d), mem_norm_g, w_mem_kv.astype(BF16))
    wq, wo = w_xq.astype(BF16), w_xo.astype(BF16)
    wu, wd = w_up.astype(BF16), w_down.astype(BF16)
    xt = x.reshape(batch * seq, d)
    for i in range(depth):
        g = norm_g[i]
        j = i // 2
        if i % 2 == 0:
            xt = mlstm_layer(xt, g[0], g[1], mlstm_w_in, j, mlstm_b_gates[j], mlstm_head_g[j],
                             mlstm_w_out, batch, seq)
        else:
            xt = gdn_layer(xt, g[0], g[1], gdn_w_in, j, gdn_conv_w[j], gdn_a_log[j], gdn_dt_bias[j],
                           gdn_norm_g[j], gdn_w_out, batch, seq)
        xt = xattn(xt, g[2], wq, kv, wo, i, g[3], seq, 512)
        xt = mlp(xt, g[4], wu, wd, i, g[5], 512, 1024)
    return xt.reshape(batch, seq, d)
```

```python
import functools

import jax
import jax.numpy as jnp
from jax import lax
from jax.experimental import pallas as pl
from jax.experimental.pallas import tpu as pltpu

EPS = 1e-6
GATE_SOFTCAP = 15.0
D_MODEL = 2048
ML_H = 4
ML_DV = D_MODEL // ML_H
ML_DQK = ML_DV // 2
ML_CHUNK = 256
GK_H = 16
GV_H = 32
GD = 128
CONV_K = 4
GDN_CHUNK = 64
GDN_HEADS_PER_STEP = 16
GDN_ROWS_PER_STEP = 256
GDN_INV_BASE = 8
MEM_LEN = 256
XA_H = 4
XA_DH = 128

LANES = 128
SUBLANES = 8
NEG_BIG = -1e30
VMEM_LIMIT = 56 * 1024 * 1024
RMS_ROWS = 16

BF16 = jnp.bfloat16
F32 = jnp.float32


def _params(sem):
    return pltpu.CompilerParams(dimension_semantics=sem, vmem_limit_bytes=VMEM_LIMIT)


def _rms(xf, g):
    ms = jnp.mean(xf * xf, axis=-1, keepdims=True)
    return xf * lax.rsqrt(ms + EPS) * g


def _rms_rows_to(dst_ref, src_ref, g_ref):
    g = g_ref[...]
    for r in range(0, src_ref.shape[0], RMS_ROWS):
        dst_ref[r:r + RMS_ROWS, :] = _rms(src_ref[r:r + RMS_ROWS, :], g).astype(dst_ref.dtype)


def _residual_rms_rows(o_ref, x_ref, g_ref):
    g = g_ref[...]
    for r in range(0, o_ref.shape[0], RMS_ROWS):
        o_ref[r:r + RMS_ROWS, :] = x_ref[r:r + RMS_ROWS, :] + _rms(o_ref[r:r + RMS_ROWS, :], g)


def _dot(a, b):
    return jnp.dot(a, b, preferred_element_type=F32)


def _dot_nt(a, b):
    return lax.dot_general(a, b, (((1,), (1,)), ((), ())), preferred_element_type=F32)


def _bmm(a, b):
    return jnp.einsum("pmk,pkn->pmn", a, b, preferred_element_type=F32)


def _bmm_nt(a, b):
    return jnp.einsum("pmk,pnk->pmn", a, b, preferred_element_type=F32)


def _bmm_tn(a, b):
    return jnp.einsum("pkm,pkn->pmn", a, b, preferred_element_type=F32)


def _split3(x):
    hi = x.astype(BF16)
    r = x - hi.astype(F32)
    mid = r.astype(BF16)
    lo = (r - mid.astype(F32)).astype(BF16)
    return hi, mid, lo


def _cumsum_rows(tri, x):
    hi, mid, lo = _split3(x)
    return _dot(tri, hi) + _dot(tri, mid) + _dot(tri, lo)


def _cumsum_cols(x, tri):
    hi, mid, lo = _split3(x)
    return _dot(hi, tri) + _dot(mid, tri) + _dot(lo, tri)


def _softplus(x):
    return jnp.maximum(x, 0.0) + jnp.log1p(jnp.exp(-jnp.abs(x)))


def _sigmoid(x):
    return 1.0 / (1.0 + jnp.exp(-x))


def _norm_matmul_kernel(x_ref, g_ref, w_ref, wg_ref, o_ref, og_ref, h_ref):
    @pl.when(pl.program_id(1) == 0)
    def _():
        _rms_rows_to(h_ref, x_ref, g_ref)
        og_ref[...] = _dot(h_ref[...], wg_ref[...])

    o_ref[...] = _dot(h_ref[...], w_ref[...]).astype(o_ref.dtype)


def norm_matmul(x, g, w, layer, n, w_gate, tm, tn):
    t, d = x.shape
    tm = min(tm, t)
    tn = min(tn, n)
    return pl.pallas_call(
        _norm_matmul_kernel,
        grid=(t // tm, n // tn),
        in_specs=[
            pl.BlockSpec((tm, d), lambda i, j: (i, 0)),
            pl.BlockSpec((1, d), lambda i, j: (0, 0)),
            pl.BlockSpec((None, d, tn), lambda i, j: (layer, 0, j)),
            pl.BlockSpec((d, LANES), lambda i, j: (0, 0)),
        ],
        out_specs=[
            pl.BlockSpec((tm, tn), lambda i, j: (i, j)),
            pl.BlockSpec((tm, LANES), lambda i, j: (i, 0)),
        ],
        out_shape=[
            jax.ShapeDtypeStruct((t, n), BF16),
            jax.ShapeDtypeStruct((t, LANES), F32),
        ],
        scratch_shapes=[pltpu.VMEM((tm, d), BF16)],
        compiler_params=_params(("parallel", "arbitrary")),
        name="norm_matmul",
    )(x, g.reshape(1, d), w, w_gate)


def _matmul_norm_res_kernel(y_ref, w_ref, g_ref, x_ref, o_ref, *, k_steps):
    if k_steps == 1:
        o_ref[...] = _dot(y_ref[...], w_ref[...])
        _residual_rms_rows(o_ref, x_ref, g_ref)
        return
    k = pl.program_id(1)

    @pl.when(k == 0)
    def _():
        o_ref[...] = _dot(y_ref[...], w_ref[...])

    @pl.when(k > 0)
    def _():
        o_ref[...] += _dot(y_ref[...], w_ref[...])

    @pl.when(k == k_steps - 1)
    def _():
        _residual_rms_rows(o_ref, x_ref, g_ref)


def matmul_norm_res(y, w, layer, g, x, tm, tk):
    t, kdim = y.shape
    d = w.shape[-1]
    tm = min(tm, t)
    tk = min(tk, kdim)
    return pl.pallas_call(
        functools.partial(_matmul_norm_res_kernel, k_steps=kdim // tk),
        grid=(t // tm, kdim // tk),
        in_specs=[
            pl.BlockSpec((tm, tk), lambda i, k: (i, k)),
            pl.BlockSpec((None, tk, d), lambda i, k: (layer, k, 0)),
            pl.BlockSpec((1, d), lambda i, k: (0, 0)),
            pl.BlockSpec((tm, d), lambda i, k: (i, 0)),
        ],
        out_specs=pl.BlockSpec((tm, d), lambda i, k: (i, 0)),
        out_shape=jax.ShapeDtypeStruct((t, d), F32),
        compiler_params=_params(("parallel", "arbitrary")),
        name="matmul_norm_res",
    )(y, w, g.reshape(1, d), x)


def _mem_kv_kernel(m_ref, g_ref, w_ref, o_ref):
    o_ref[...] = _dot(_rms(m_ref[...], g_ref[...]).astype(BF16), w_ref[...]).astype(o_ref.dtype)


def mem_kv(mem2d, g, w):
    t, d = mem2d.shape
    n = w.shape[1]
    tm = min(256, t)
    return pl.pallas_call(
        _mem_kv_kernel,
        grid=(t // tm,),
        in_specs=[
            pl.BlockSpec((tm, d), lambda i: (i, 0)),
            pl.BlockSpec((1, d), lambda i: (0, 0)),
            pl.BlockSpec((d, n), lambda i: (0, 0)),
        ],
        out_specs=pl.BlockSpec((tm, n), lambda i: (i, 0)),
        out_shape=jax.ShapeDtypeStruct((t, n), BF16),
        compiler_params=_params(("parallel",)),
        name="mem_kv",
    )(mem2d, g.reshape(1, d), w)


def _xattn_kernel(x_ref, g_pre_ref, wq_ref, kv_ref, wo_ref, g_post_ref, o_ref, h_ref):
    _rms_rows_to(h_ref, x_ref, g_pre_ref)
    q = _dot(h_ref[...], wq_ref[...]).astype(BF16)
    kv = kv_ref[...]
    qh = jnp.stack([q[:, a * XA_DH:(a + 1) * XA_DH] for a in range(XA_H)])
    kh = jnp.stack([kv[:, a * XA_DH:(a + 1) * XA_DH] for a in range(XA_H)])
    vh = jnp.stack([kv[:, (XA_H + a) * XA_DH:(XA_H + a + 1) * XA_DH] for a in range(XA_H)])
    s = _bmm_nt(qh, kh) * (XA_DH ** -0.5)
    s = s - jnp.max(s, axis=-1, keepdims=True)
    e = jnp.exp(s)
    p = e / jnp.sum(e, axis=-1, keepdims=True)
    oh = _bmm(p.astype(BF16), vh).astype(BF16)
    o = jnp.concatenate([oh[a] for a in range(XA_H)], axis=-1)
    o_ref[...] = _dot(o, wo_ref[...])
    _residual_rms_rows(o_ref, x_ref, g_post_ref)


def xattn(x, g_pre, wq, kv, wo, layer, g_post, seq, tm):
    t, d = x.shape
    tm = min(tm, seq)
    blocks_per_batch = seq // tm
    nq = wq.shape[-1]
    return pl.pallas_call(
        _xattn_kernel,
        grid=(t // tm,),
        in_specs=[
            pl.BlockSpec((tm, d), lambda i: (i, 0)),
            pl.BlockSpec((1, d), lambda i: (0, 0)),
            pl.BlockSpec((None, d, nq), lambda i: (layer, 0, 0)),
            pl.BlockSpec((MEM_LEN, 2 * nq), lambda i: (i // blocks_per_batch, 0)),
            pl.BlockSpec((None, nq, d), lambda i: (layer, 0, 0)),
            pl.BlockSpec((1, d), lambda i: (0, 0)),
        ],
        out_specs=pl.BlockSpec((tm, d), lambda i: (i, 0)),
        out_shape=jax.ShapeDtypeStruct((t, d), F32),
        scratch_shapes=[pltpu.VMEM((tm, d), BF16)],
        compiler_params=_params(("parallel",)),
        name="xattn",
    )(x, g_pre.reshape(1, d), wq, kv, wo, g_post.reshape(1, d))


def _mlp_kernel(x_ref, g_pre_ref, wu_ref, wd_ref, g_post_ref, o_ref, h_ref):
    j = pl.program_id(1)

    def partial_sum():
        u = jnp.maximum(_dot(h_ref[...], wu_ref[...]), 0.0)
        return _dot((u * u).astype(BF16), wd_ref[...])

    @pl.when(j == 0)
    def _():
        _rms_rows_to(h_ref, x_ref, g_pre_ref)
        o_ref[...] = partial_sum()

    @pl.when(j > 0)
    def _():
        o_ref[...] += partial_sum()

    @pl.when(j == pl.num_programs(1) - 1)
    def _():
        _residual_rms_rows(o_ref, x_ref, g_post_ref)


def mlp(x, g_pre, wu, wd, layer, g_post, tm, tf):
    t, d = x.shape
    f = wu.shape[-1]
    tm = min(tm, t)
    return pl.pallas_call(
        _mlp_kernel,
        grid=(t // tm, f // tf),
        in_specs=[
            pl.BlockSpec((tm, d), lambda i, j: (i, 0)),
            pl.BlockSpec((1, d), lambda i, j: (0, 0)),
            pl.BlockSpec((None, d, tf), lambda i, j: (layer, 0, j)),
            pl.BlockSpec((None, tf, d), lambda i, j: (layer, j, 0)),
            pl.BlockSpec((1, d), lambda i, j: (0, 0)),
        ],
        out_specs=pl.BlockSpec((tm, d), lambda i, j: (i, 0)),
        out_shape=jax.ShapeDtypeStruct((t, d), F32),
        scratch_shapes=[pltpu.VMEM((tm, d), BF16)],
        compiler_params=_params(("parallel", "arbitrary")),
        name="mlp",
    )(x, g_pre.reshape(1, d), wu, wd, g_post.reshape(1, d))


def _mlstm_kernel(p_ref, gc_ref, gr_ref, bc_ref, br_ref, hg_ref, y_ref, c_ref, n_ref, m_ref):
    L = p_ref.shape[0]
    H = ML_H

    @pl.when(pl.program_id(1) == 0)
    def _():
        c_ref[...] = jnp.zeros_like(c_ref)
        n_ref[...] = jnp.zeros_like(n_ref)
        m_ref[...] = jnp.zeros_like(m_ref)

    row = lax.broadcasted_iota(jnp.int32, (L, L), 0)
    col = lax.broadcasted_iota(jnp.int32, (L, L), 1)
    causal = row >= col
    tril = causal.astype(BF16)
    triu = (row <= col).astype(BF16)

    gc = GATE_SOFTCAP * jnp.tanh((gc_ref[...] + bc_ref[...]) / GATE_SOFTCAP)
    bcum_c_all = _cumsum_rows(tril, -_softplus(-gc))
    gr = GATE_SOFTCAP * jnp.tanh((gr_ref[...] + br_ref[...]) / GATE_SOFTCAP)
    bcum_r_all = _cumsum_cols(-_softplus(-gr), triu)

    qo, ko, vo, oo = 0, H * ML_DQK, 2 * H * ML_DQK, 2 * H * ML_DQK + H * ML_DV
    q = jnp.stack([p_ref[:, qo + h * ML_DQK: qo + (h + 1) * ML_DQK] for h in range(H)]) * (ML_DQK ** -0.5)
    k = jnp.stack([p_ref[:, ko + h * ML_DQK: ko + (h + 1) * ML_DQK] for h in range(H)])
    v = jnp.stack([p_ref[:, vo + h * ML_DV: vo + (h + 1) * ML_DV] for h in range(H)])
    ig_c = jnp.stack([gc[:, h:h + 1] for h in range(H)])
    ig_r = jnp.stack([gr[h:h + 1, :] for h in range(H)])
    bcum_c = jnp.stack([bcum_c_all[:, H + h:H + h + 1] for h in range(H)])
    bcum_r = jnp.stack([bcum_r_all[H + h:H + h + 1, :] for h in range(H)])
    b_end = bcum_r[:, :, L - 1:L]
    m_prev = m_ref[...]
    c_prev = c_ref[...]
    n_prev = n_ref[...]

    dmat = jnp.where(causal, bcum_c - bcum_r + ig_r, NEG_BIG)
    inter = bcum_c + m_prev
    mt = jnp.maximum(inter, jnp.max(dmat, axis=-1, keepdims=True))
    a_inter = jnp.exp(inter - mt)
    pm = _bmm_nt(q, k) * jnp.exp(dmat - mt)
    num = a_inter * _bmm(q, c_prev.astype(BF16)) + _bmm(pm.astype(BF16), v)
    den = (a_inter * jnp.sum(q.astype(F32) * n_prev, axis=-1, keepdims=True)
           + jnp.sum(pm, axis=-1, keepdims=True))
    out = num * (1.0 / jnp.maximum(jnp.abs(den), jnp.exp(-mt)))

    m_new = jnp.maximum(b_end + m_prev, jnp.max(b_end - bcum_r + ig_r, axis=-1, keepdims=True))
    a_state = jnp.exp(b_end + m_prev - m_new)
    wk = k.astype(F32) * jnp.exp(b_end - bcum_c + ig_c - m_new)
    c_ref[...] = a_state * c_prev + _bmm_tn(wk.astype(BF16), v)
    n_ref[...] = a_state * n_prev + jnp.sum(wk, axis=1, keepdims=True)
    m_ref[...] = m_new

    for h in range(H):
        hs = _rms(out[h], hg_ref[:, h * ML_DV:(h + 1) * ML_DV])
        og = p_ref[:, oo + h * ML_DV: oo + (h + 1) * ML_DV].astype(F32)
        y_ref[:, h * ML_DV:(h + 1) * ML_DV] = (hs * _sigmoid(og)).astype(y_ref.dtype)


def mlstm_scan(p, gates_col, gates_row, bias_col, bias_row, head_g, batch, seq):
    t = p.shape[0]
    L = min(ML_CHUNK, seq)
    nc = seq // L
    return pl.pallas_call(
        _mlstm_kernel,
        grid=(batch, nc),
        in_specs=[
            pl.BlockSpec((L, p.shape[1]), lambda b, c: (b * nc + c, 0)),
            pl.BlockSpec((L, LANES), lambda b, c: (b * nc + c, 0)),
            pl.BlockSpec((2 * ML_H, L), lambda b, c: (0, b * nc + c)),
            pl.BlockSpec((1, LANES), lambda b, c: (0, 0)),
            pl.BlockSpec((2 * ML_H, 1), lambda b, c: (0, 0)),
            pl.BlockSpec((1, ML_H * ML_DV), lambda b, c: (0, 0)),
        ],
        out_specs=pl.BlockSpec((L, ML_H * ML_DV), lambda b, c: (b * nc + c, 0)),
        out_shape=jax.ShapeDtypeStruct((t, ML_H * ML_DV), BF16),
        scratch_shapes=[
            pltpu.VMEM((ML_H, ML_DQK, ML_DV), F32),
            pltpu.VMEM((ML_H, 1, ML_DQK), F32),
            pltpu.VMEM((ML_H, 1, 1), F32),
        ],
        compiler_params=_params(("parallel", "arbitrary")),
        name="mlstm_scan",
    )(p, gates_col, gates_row, bias_col, bias_row, head_g)


def _gdn_inproj_kernel(x_ref, g_ref, w_ref, wg_ref, cw_ref, o_ref, og_ref, h_ref, acc_ref, tail_ref, *,
                       blocks_per_batch, n_q, n_qk, n_qkv, dot_rows, conv_rows):
    i = pl.program_id(0)
    j = pl.program_id(1)
    tm, tn = o_ref.shape

    @pl.when(j == 0)
    def _():
        _rms_rows_to(h_ref, x_ref, g_ref)
        og_ref[...] = _dot(h_ref[...], wg_ref[...])

    def conv_tile(l2):
        first = (i % blocks_per_batch) == 0
        acc_ref[0:SUBLANES, :] = jnp.where(first, 0.0, tail_ref[j])
        cw = cw_ref[...]
        scale = jnp.where(j < n_q, GD ** -0.5, 1.0)
        for k0 in range(0, tm, dot_rows):
            acc_ref[SUBLANES + k0: SUBLANES + k0 + dot_rows, :] = _dot(h_ref[k0:k0 + dot_rows, :], w_ref[...])
            for r0 in range(k0, k0 + dot_rows, conv_rows):
                a = acc_ref[SUBLANES + r0: SUBLANES + r0 + conv_rows, :] * cw[CONV_K - 1:CONV_K, :]
                for s in range(1, CONV_K):
                    a = a + (acc_ref[SUBLANES - s + r0: SUBLANES - s + r0 + conv_rows, :]
                             * cw[CONV_K - 1 - s:CONV_K - s, :])
                half = 0.5 * a
                y = half + half * jnp.tanh(half)
                if l2:
                    for c in range(tn // GD):
                        blk = y[:, c * GD:(c + 1) * GD]
                        ss = jnp.sum(blk * blk, axis=-1, keepdims=True)
                        o_ref[r0:r0 + conv_rows, c * GD:(c + 1) * GD] = (
                            blk * (lax.rsqrt(ss + EPS) * scale)).astype(o_ref.dtype)
                else:
                    o_ref[r0:r0 + conv_rows, :] = y.astype(o_ref.dtype)
        tail_ref[j] = acc_ref[tm:tm + SUBLANES, :]

    @pl.when(j < n_qk)
    def _():
        conv_tile(True)

    @pl.when((j >= n_qk) & (j < n_qkv))
    def _():
        conv_tile(False)

    @pl.when(j >= n_qkv)
    def _():
        o_ref[...] = _dot(h_ref[...], w_ref[...]).astype(o_ref.dtype)


def gdn_inproj(x, g, w, layer, w_gate, conv_w, n, seq, tm, tn):
    t, d = x.shape
    n_qkv = conv_w.shape[1]
    tm = min(tm, seq)
    kern = functools.partial(
        _gdn_inproj_kernel, blocks_per_batch=seq // tm, n_q=GK_H * GD // tn, n_qk=2 * GK_H * GD // tn,
        n_qkv=n_qkv // tn, dot_rows=min(256, tm), conv_rows=32)
    last_conv_tile = n_qkv // tn - 1
    return pl.pallas_call(
        kern,
        grid=(t // tm, n // tn),
        in_specs=[
            pl.BlockSpec((tm, d), lambda i, j: (i, 0)),
            pl.BlockSpec((1, d), lambda i, j: (0, 0)),
            pl.BlockSpec((None, d, tn), lambda i, j: (layer, 0, j)),
            pl.BlockSpec((d, LANES), lambda i, j: (0, 0)),
            pl.BlockSpec((CONV_K, tn), lambda i, j: (0, jnp.minimum(j, last_conv_tile))),
        ],
        out_specs=[
            pl.BlockSpec((tm, tn), lambda i, j: (i, j)),
            pl.BlockSpec((tm, LANES), lambda i, j: (i, 0)),
        ],
        out_shape=[
            jax.ShapeDtypeStruct((t, n), BF16),
            jax.ShapeDtypeStruct((t, LANES), F32),
        ],
        scratch_shapes=[
            pltpu.VMEM((tm, d), BF16),
            pltpu.VMEM((tm + SUBLANES, tn), F32),
            pltpu.VMEM((n_qkv // tn, SUBLANES, tn), F32),
        ],
        compiler_params=_params(("arbitrary", "arbitrary")),
        name="gdn_inproj",
    )(x, g.reshape(1, d), w, w_gate, conv_w)


def _pair_blockdiag(y):
    L = y.shape[1]
    left = lax.broadcasted_iota(jnp.int32, (L, 2 * L), 1) < L
    zero = jnp.zeros_like(y)
    return jnp.concatenate([jnp.where(left, y, zero), jnp.where(left, zero, y)], axis=1)


def _pair_mm(x, y):
    return _bmm(x, _pair_blockdiag(y))


def _unit_lower_inverse_pairs(a, eye, same_block):
    L = a.shape[1]
    d = jnp.where(same_block[GDN_INV_BASE], a, 0.0)
    pw = (-d).astype(BF16)
    t = eye - d
    for _ in range(GDN_INV_BASE.bit_length() - 2):
        pw = _pair_mm(pw, pw).astype(BF16)
        t = t + _pair_mm(t.astype(BF16), pw)
    size = GDN_INV_BASE
    while size < L:
        off = jnp.where(same_block[2 * size] & jnp.logical_not(same_block[size]), a, 0.0)
        x = _pair_mm(off.astype(BF16), t.astype(BF16))
        t = t - _pair_mm(t.astype(BF16), x.astype(BF16))
        size *= 2
    return t


def _gdn_kernel(q_ref, k_ref, v_ref, z_ref, gc_ref, gr_ref, pc_ref, pr_ref, ng_ref, y_ref, s_ref):
    G = GDN_HEADS_PER_STEP
    L = GDN_CHUNK
    rows = q_ref.shape[0]
    nch = rows // L
    assert 2 * L == GD

    @pl.when(pl.program_id(2) == 0)
    def _():
        s_ref[...] = jnp.zeros_like(s_ref)

    row = lax.broadcasted_iota(jnp.int32, (L, 2 * L), 0)
    lane = lax.broadcasted_iota(jnp.int32, (L, 2 * L), 1)
    col = lane & (L - 1)
    left = lane < L
    causal = row >= col
    strict = row > col
    eye = (row == col).astype(F32)
    same_block = {}
    size = GDN_INV_BASE
    while size <= L:
        shift = size.bit_length() - 1
        same_block[size] = (row >> shift) == (col >> shift)
        size *= 2
    triu = (lax.broadcasted_iota(jnp.int32, (L, L), 0) <= lax.broadcasted_iota(jnp.int32, (L, L), 1)).astype(BF16)
    brow = lax.broadcasted_iota(jnp.int32, (rows, rows), 0)
    bcol = lax.broadcasted_iota(jnp.int32, (rows, rows), 1)
    lshift = L.bit_length() - 1
    chunk_tril = (((brow >> lshift) == (bcol >> lshift)) & (brow >= bcol)).astype(BF16)

    gcol = gc_ref[...]
    pc = pc_ref[...]
    beta_all = _sigmoid(gcol)
    g_c = -jnp.exp(pc[0:1, :]) * _softplus(gcol + pc[1:2, :])
    gam_c_all = _cumsum_rows(chunk_tril, g_c)
    pr = pr_ref[...]
    g_r = -jnp.exp(pr[:, 0:1]) * _softplus(gr_ref[...] + pr[:, 1:2])
    gam_r_all = _cumsum_cols(g_r.reshape(nch * 2 * G, L), triu).reshape(nch, 2 * G, L)

    order = [(cc, j) for cc in range(nch) for j in range(G)]
    pairs = [(cc, kh) for cc in range(nch) for kh in range(G // 2)]

    def rows_of(cc):
        return slice(cc * L, (cc + 1) * L)

    def lanes_of(h):
        return slice(h * GD, (h + 1) * GD)

    def to_pairs(x):
        return jnp.stack([jnp.where(left, x[2 * i], x[2 * i + 1]) for i in range(len(pairs))])

    def per_value_head(x):
        return jnp.stack([x[i // 2] for i in range(len(order))])

    def pad_rows(x):
        zero = jnp.zeros_like(x[0])
        return jnp.stack([jnp.concatenate([x[i], zero] if i % 2 == 0 else [zero, x[i]], axis=0)
                          for i in range(x.shape[0])])

    beta_b = jnp.stack([jnp.broadcast_to(beta_all[rows_of(cc), j:j + 1], (L, GD)) for cc, j in order])
    gam_b = jnp.stack([jnp.broadcast_to(gam_c_all[rows_of(cc), G + j:G + j + 1], (L, GD)) for cc, j in order])
    gam_last = jnp.stack([gam_r_all[cc, G + j:G + j + 1, L - 1:L] for cc, j in order])
    gam_r_p = jnp.stack([
        jnp.concatenate([gam_r_all[cc, G + 2 * kh:G + 2 * kh + 1, :],
                         gam_r_all[cc, G + 2 * kh + 1:G + 2 * kh + 2, :]], axis=-1)
        for cc, kh in pairs])

    k_k = jnp.stack([k_ref[rows_of(cc), lanes_of(kh)] for cc, kh in pairs])
    q_k = jnp.stack([q_ref[rows_of(cc), lanes_of(kh)] for cc, kh in pairs])
    k_twice = jnp.concatenate([k_k, k_k], axis=1)
    kk_p = _bmm_nt(k_k, k_twice)
    qk_p = _bmm_nt(q_k, k_twice)

    e = jnp.exp(jnp.where(causal, to_pairs(gam_b) - gam_r_p, NEG_BIG))
    a = to_pairs(beta_b) * kk_p * jnp.where(strict, e, 0.0)
    aqk = per_value_head((qk_p * e).astype(BF16))
    tinv = per_value_head(_unit_lower_inverse_pairs(a, eye, same_block).astype(BF16))

    kf = per_value_head(k_k).astype(F32)
    qf = per_value_head(q_k).astype(F32)
    v = jnp.stack([v_ref[rows_of(cc), lanes_of(j)] for cc, j in order]).astype(F32)
    egam_b = jnp.exp(gam_b)
    rhs = jnp.concatenate([v * beta_b, kf * beta_b * egam_b], axis=-1).astype(BF16)
    uw = _bmm(tinv, pad_rows(rhs))
    u = uw[:, :, :GD]
    lhs = jnp.concatenate([uw[:, :, GD:].astype(BF16), (qf * egam_b).astype(BF16)], axis=1)
    kd = (kf * jnp.exp(gam_last - gam_b)).astype(BF16)
    dl = jnp.exp(gam_last)

    s = s_ref[...]
    ng = ng_ref[...]
    for cc in range(nch):
        sl = slice(cc * G, (cc + 1) * G)
        ws_qs = _bmm(lhs[sl], s.astype(BF16))
        vn = (u[sl] - ws_qs[:, :L, :]).astype(BF16)
        o = ws_qs[:, L:, :] + _bmm(aqk[sl], pad_rows(vn))
        s = dl[sl] * s + _bmm_tn(kd[sl], vn)
        hs = _rms(o, ng)
        for j in range(G):
            z = z_ref[rows_of(cc), lanes_of(j)].astype(F32)
            y_ref[rows_of(cc), lanes_of(j)] = (hs[j] * (z * _sigmoid(z))).astype(y_ref.dtype)
    s_ref[...] = s


def gdn_scan(p, z_col0, gates_col, gates_row, par_col, par_row, norm_g, batch, seq):
    t = p.shape[0]
    G = GDN_HEADS_PER_STEP
    L = GDN_CHUNK
    rows = min(GDN_ROWS_PER_STEP, seq)
    nrb = seq // rows
    nch = rows // L
    ng = GV_H // G
    qw = (G // 2) * GD
    vw = G * GD
    k_blk0 = GK_H * GD // qw
    v_blk0 = 2 * GK_H * GD // vw
    z_blk0 = z_col0 // vw
    return pl.pallas_call(
        _gdn_kernel,
        grid=(batch, ng, nrb),
        in_specs=[
            pl.BlockSpec((rows, qw), lambda b, g, c: (b * nrb + c, g)),
            pl.BlockSpec((rows, qw), lambda b, g, c: (b * nrb + c, k_blk0 + g)),
            pl.BlockSpec((rows, vw), lambda b, g, c: (b * nrb + c, v_blk0 + g)),
            pl.BlockSpec((rows, vw), lambda b, g, c: (b * nrb + c, z_blk0 + g)),
            pl.BlockSpec((None, rows, 2 * G), lambda b, g, c: (g, b * nrb + c, 0)),
            pl.BlockSpec((None, nch, 2 * G, L), lambda b, g, c: (g, b * nrb + c, 0, 0)),
            pl.BlockSpec((None, 2, 2 * G), lambda b, g, c: (g, 0, 0)),
            pl.BlockSpec((None, 2 * G, 2), lambda b, g, c: (g, 0, 0)),
            pl.BlockSpec((1, GD), lambda b, g, c: (0, 0)),
        ],
        out_specs=pl.BlockSpec((rows, vw), lambda b, g, c: (b * nrb + c, g)),
        out_shape=jax.ShapeDtypeStruct((t, GV_H * GD), BF16),
        scratch_shapes=[pltpu.VMEM((G, GD, GD), F32)],
        compiler_params=_params(("parallel", "parallel", "arbitrary")),
        name="gdn_scan",
    )(p, p, p, p, gates_col, gates_row, par_col, par_row, norm_g.reshape(1, GD))


def _gate_weights(w_in_layer, n_main):
    w = w_in_layer[:, n_main:]
    return jnp.pad(w, ((0, 0), (0, LANES - w.shape[1]))).astype(BF16)


def mlstm_layer(x, g_pre, g_post, w_in, layer, b_gates, head_g, w_out, batch, seq):
    n_main = 2 * ML_H * ML_DQK + 2 * ML_H * ML_DV
    p, gates_col = norm_matmul(x, g_pre, w_in.astype(BF16), layer, n_main,
                               _gate_weights(w_in[layer], n_main), 1024, 1024)
    gates_row = gates_col[:, :2 * ML_H].T
    bias_col = jnp.pad(b_gates.reshape(1, -1), ((0, 0), (0, LANES - 2 * ML_H)))
    bias_row = b_gates.reshape(-1, 1)
    y = mlstm_scan(p, gates_col, gates_row, bias_col, bias_row, head_g.reshape(1, -1), batch, seq)
    return matmul_norm_res(y, w_out.astype(BF16), layer, g_post, x, 512, 2048)


def gdn_layer(x, g_pre, g_post, w_in, layer, conv_w, a_log, dt_bias, norm_g, w_out, batch, seq):
    G = GDN_HEADS_PER_STEP
    ng = GV_H // G
    t = x.shape[0]
    n_qkv = 2 * GK_H * GD + GV_H * GD
    n_main = n_qkv + GV_H * GD
    p, gates = gdn_inproj(x, g_pre, w_in.astype(BF16), layer, _gate_weights(w_in[layer], n_main), conv_w,
                          n_main, seq, 1024, 1024)
    gates = gates[:, :2 * GV_H]
    gates_col = gates.reshape(t, 2, ng, G).transpose(2, 0, 1, 3).reshape(ng, t, 2 * G)
    gates_row = gates_col.reshape(ng, t // GDN_CHUNK, GDN_CHUNK, 2 * G).transpose(0, 1, 3, 2)
    zeros = jnp.zeros((ng, G), F32)
    a_grp = jnp.concatenate([zeros, a_log.reshape(ng, G)], axis=-1)
    dt_grp = jnp.concatenate([zeros, dt_bias.reshape(ng, G)], axis=-1)
    par_col = jnp.stack([a_grp, dt_grp], axis=1)
    par_row = par_col.transpose(0, 2, 1)
    y = gdn_scan(p, n_qkv, gates_col, gates_row, par_col, par_row, norm_g, batch, seq)
    return matmul_norm_res(y, w_out.astype(BF16), layer, g_post, x, 512, 2048)


def kernel(x, mem, norm_g, mem_norm_g, w_mem_kv, w_xq, w_xo, w_up, w_down, mlstm_w_in, mlstm_b_gates, mlstm_head_g, mlstm_w_out, gdn_w_in, gdn_conv_w, gdn_a_log, gdn_dt_bias, gdn_norm_g, gdn_w_out):
    batch, seq, d = x.shape
    depth = norm_g.shape[0]
    kv = mem_kv(mem.reshape(batch * MEM_LEN, d), mem_norm_g, w_mem_kv.astype(BF16))
    wq, wo = w_xq.astype(BF16), w_xo.astype(BF16)
    wu, wd = w_up.astype(BF16), w_down.astype(BF16)
    xt = x.reshape(batch * seq, d)
    for i in range(depth):
        g = norm_g[i]
        j = i // 2
        if i % 2 == 0:
            xt = mlstm_layer(xt, g[0], g[1], mlstm_w_in, j, mlstm_b_gates[j], mlstm_head_g[j],
                             mlstm_w_out, batch, seq)
        else:
            xt = gdn_layer(xt, g[0], g[1], gdn_w_in, j, gdn_conv_w[j], gdn_a_log[j], gdn_dt_bias[j],
                           gdn_norm_g[j], gdn_w_out, batch, seq)
        xt = xattn(xt, g[2], wq, kv, wo, i, g[3], seq, 512)
        xt = mlp(xt, g[4], wu, wd, i, g[5], 512, 1024)
    return xt.reshape(batch, seq, d)
```

```python
import functools

import jax
import jax.numpy as jnp
from jax import lax
from jax.experimental import pallas as pl
from jax.experimental.pallas import tpu as pltpu

EPS = 1e-6
GATE_SOFTCAP = 15.0
D_MODEL = 2048
ML_H = 4
ML_DV = D_MODEL // ML_H
ML_DQK = ML_DV // 2
ML_CHUNK = 256
GK_H = 16
GV_H = 32
GD = 128
CONV_K = 4
GDN_CHUNK = 64
GDN_HEADS_PER_STEP = 16
GDN_ROWS_PER_STEP = 256
GDN_INV_BASE = 8
MEM_LEN = 256
XA_H = 4
XA_DH = 128

LANES = 128
SUBLANES = 8
NEG_BIG = -1e30
VMEM_LIMIT = 56 * 1024 * 1024
RMS_ROWS = 16

BF16 = jnp.bfloat16
F32 = jnp.float32


def _params(sem):
    return pltpu.CompilerParams(dimension_semantics=sem, vmem_limit_bytes=VMEM_LIMIT)


def _rms(xf, g):
    ms = jnp.mean(xf * xf, axis=-1, keepdims=True)
    return xf * lax.rsqrt(ms + EPS) * g


def _rms_rows_to(dst_ref, src_ref, g_ref):
    g = g_ref[...]
    for r in range(0, src_ref.shape[0], RMS_ROWS):
        dst_ref[r:r + RMS_ROWS, :] = _rms(src_ref[r:r + RMS_ROWS, :], g).astype(dst_ref.dtype)


def _residual_rms_rows(o_ref, x_ref, g_ref):
    g = g_ref[...]
    for r in range(0, o_ref.shape[0], RMS_ROWS):
        o_ref[r:r + RMS_ROWS, :] = x_ref[r:r + RMS_ROWS, :] + _rms(o_ref[r:r + RMS_ROWS, :], g)


def _dot(a, b):
    return jnp.dot(a, b, preferred_element_type=F32)


def _dot_nt(a, b):
    return lax.dot_general(a, b, (((1,), (1,)), ((), ())), preferred_element_type=F32)


def _bmm(a, b):
    return jnp.einsum("pmk,pkn->pmn", a, b, preferred_element_type=F32)


def _bmm_nt(a, b):
    return jnp.einsum("pmk,pnk->pmn", a, b, preferred_element_type=F32)


def _bmm_tn(a, b):
    return jnp.einsum("pkm,pkn->pmn", a, b, preferred_element_type=F32)


def _split3(x):
    hi = x.astype(BF16)
    r = x - hi.astype(F32)
    mid = r.astype(BF16)
    lo = (r - mid.astype(F32)).astype(BF16)
    return hi, mid, lo


def _cumsum_rows(tri, x):
    hi, mid, lo = _split3(x)
    return _dot(tri, hi) + _dot(tri, mid) + _dot(tri, lo)


def _cumsum_cols(x, tri):
    hi, mid, lo = _split3(x)
    return _dot(hi, tri) + _dot(mid, tri) + _dot(lo, tri)


def _softplus(x):
    return jnp.maximum(x, 0.0) + jnp.log1p(jnp.exp(-jnp.abs(x)))


def _sigmoid(x):
    return 1.0 / (1.0 + jnp.exp(-x))


def _norm_matmul_kernel(x_ref, g_ref, w_ref, wg_ref, o_ref, og_ref, h_ref):
    @pl.when(pl.program_id(1) == 0)
    def _():
        _rms_rows_to(h_ref, x_ref, g_ref)
        og_ref[...] = _dot(h_ref[...], wg_ref[...])

    o_ref[...] = _dot(h_ref[...], w_ref[...]).astype(o_ref.dtype)


def norm_matmul(x, g, w, layer, n, w_gate, tm, tn):
    t, d = x.shape
    tm = min(tm, t)
    tn = min(tn, n)
    return pl.pallas_call(
        _norm_matmul_kernel,
        grid=(t // tm, n // tn),
        in_specs=[
            pl.BlockSpec((tm, d), lambda i, j: (i, 0)),
            pl.BlockSpec((1, d), lambda i, j: (0, 0)),
            pl.BlockSpec((None, d, tn), lambda i, j: (layer, 0, j)),
            pl.BlockSpec((d, LANES), lambda i, j: (0, 0)),
        ],
        out_specs=[
            pl.BlockSpec((tm, tn), lambda i, j: (i, j)),
            pl.BlockSpec((tm, LANES), lambda i, j: (i, 0)),
        ],
        out_shape=[
            jax.ShapeDtypeStruct((t, n), BF16),
            jax.ShapeDtypeStruct((t, LANES), F32),
        ],
        scratch_shapes=[pltpu.VMEM((tm, d), BF16)],
        compiler_params=_params(("parallel", "arbitrary")),
        name="norm_matmul",
    )(x, g.reshape(1, d), w, w_gate)


def _matmul_norm_res_kernel(y_ref, w_ref, g_ref, x_ref, o_ref, *, k_steps):
    if k_steps == 1:
        o_ref[...] = _dot(y_ref[...], w_ref[...])
        _residual_rms_rows(o_ref, x_ref, g_ref)
        return
    k = pl.program_id(1)

    @pl.when(k == 0)
    def _():
        o_ref[...] = _dot(y_ref[...], w_ref[...])

    @pl.when(k > 0)
    def _():
        o_ref[...] += _dot(y_ref[...], w_ref[...])

    @pl.when(k == k_steps - 1)
    def _():
        _residual_rms_rows(o_ref, x_ref, g_ref)


def matmul_norm_res(y, w, layer, g, x, tm, tk):
    t, kdim = y.shape
    d = w.shape[-1]
    tm = min(tm, t)
    tk = min(tk, kdim)
    return pl.pallas_call(
        functools.partial(_matmul_norm_res_kernel, k_steps=kdim // tk),
        grid=(t // tm, kdim // tk),
        in_specs=[
            pl.BlockSpec((tm, tk), lambda i, k: (i, k)),
            pl.BlockSpec((None, tk, d), lambda i, k: (layer, k, 0)),
            pl.BlockSpec((1, d), lambda i, k: (0, 0)),
            pl.BlockSpec((tm, d), lambda i, k: (i, 0)),
        ],
        out_specs=pl.BlockSpec((tm, d), lambda i, k: (i, 0)),
        out_shape=jax.ShapeDtypeStruct((t, d), F32),
        compiler_params=_params(("parallel", "arbitrary")),
        name="matmul_norm_res",
    )(y, w, g.reshape(1, d), x)


def _mem_kv_kernel(m_ref, g_ref, w_ref, o_ref):
    o_ref[...] = _dot(_rms(m_ref[...], g_ref[...]).astype(BF16), w_ref[...]).astype(o_ref.dtype)


def mem_kv(mem2d, g, w):
    t, d = mem2d.shape
    n = w.shape[1]
    tm = min(256, t)
    return pl.pallas_call(
        _mem_kv_kernel,
        grid=(t // tm,),
        in_specs=[
            pl.BlockSpec((tm, d), lambda i: (i, 0)),
            pl.BlockSpec((1, d), lambda i: (0, 0)),
            pl.BlockSpec((d, n), lambda i: (0, 0)),
        ],
        out_specs=pl.BlockSpec((tm, n), lambda i: (i, 0)),
        out_shape=jax.ShapeDtypeStruct((t, n), BF16),
        compiler_params=_params(("parallel",)),
        name="mem_kv",
    )(mem2d, g.reshape(1, d), w)


def _xattn_kernel(x_ref, g_pre_ref, wq_ref, kv_ref, wo_ref, g_post_ref, o_ref, h_ref):
    _rms_rows_to(h_ref, x_ref, g_pre_ref)
    q = _dot(h_ref[...], wq_ref[...]).astype(BF16)
    kv = kv_ref[...]
    qh = jnp.stack([q[:, a * XA_DH:(a + 1) * XA_DH] for a in range(XA_H)])
    kh = jnp.stack([kv[:, a * XA_DH:(a + 1) * XA_DH] for a in range(XA_H)])
    vh = jnp.stack([kv[:, (XA_H + a) * XA_DH:(XA_H + a + 1) * XA_DH] for a in range(XA_H)])
    s = _bmm_nt(qh, kh) * (XA_DH ** -0.5)
    s = s - jnp.max(s, axis=-1, keepdims=True)
    e = jnp.exp(s)
    p = e / jnp.sum(e, axis=-1, keepdims=True)
    oh = _bmm(p.astype(BF16), vh).astype(BF16)
    o = jnp.concatenate([oh[a] for a in range(XA_H)], axis=-1)
    o_ref[...] = _dot(o, wo_ref[...])
    _residual_rms_rows(o_ref, x_ref, g_post_ref)


def xattn(x, g_pre, wq, kv, wo, layer, g_post, seq, tm):
    t, d = x.shape
    tm = min(tm, seq)
    blocks_per_batch = seq // tm
    nq = wq.shape[-1]
    return pl.pallas_call(
        _xattn_kernel,
        grid=(t // tm,),
        in_specs=[
            pl.BlockSpec((tm, d), lambda i: (i, 0)),
            pl.BlockSpec((1, d), lambda i: (0, 0)),
            pl.BlockSpec((None, d, nq), lambda i: (layer, 0, 0)),
            pl.BlockSpec((MEM_LEN, 2 * nq), lambda i: (i // blocks_per_batch, 0)),
            pl.BlockSpec((None, nq, d), lambda i: (layer, 0, 0)),
            pl.BlockSpec((1, d), lambda i: (0, 0)),
        ],
        out_specs=pl.BlockSpec((tm, d), lambda i: (i, 0)),
        out_shape=jax.ShapeDtypeStruct((t, d), F32),
        scratch_shapes=[pltpu.VMEM((tm, d), BF16)],
        compiler_params=_params(("parallel",)),
        name="xattn",
    )(x, g_pre.reshape(1, d), wq, kv, wo, g_post.reshape(1, d))


def _mlp_kernel(x_ref, g_pre_ref, wu_ref, wd_ref, g_post_ref, o_ref, h_ref):
    j = pl.program_id(1)

    def partial_sum():
        u = jnp.maximum(_dot(h_ref[...], wu_ref[...]), 0.0)
        return _dot((u * u).astype(BF16), wd_ref[...])

    @pl.when(j == 0)
    def _():
        _rms_rows_to(h_ref, x_ref, g_pre_ref)
        o_ref[...] = partial_sum()

    @pl.when(j > 0)
    def _():
        o_ref[...] += partial_sum()

    @pl.when(j == pl.num_programs(1) - 1)
    def _():
        _residual_rms_rows(o_ref, x_ref, g_post_ref)


def mlp(x, g_pre, wu, wd, layer, g_post, tm, tf):
    t, d = x.shape
    f = wu.shape[-1]
    tm = min(tm, t)
    return pl.pallas_call(
        _mlp_kernel,
        grid=(t // tm, f // tf),
        in_specs=[
            pl.BlockSpec((tm, d), lambda i, j: (i, 0)),
            pl.BlockSpec((1, d), lambda i, j: (0, 0)),
            pl.BlockSpec((None, d, tf), lambda i, j: (layer, 0, j)),
            pl.BlockSpec((None, tf, d), lambda i, j: (layer, j, 0)),
            pl.BlockSpec((1, d), lambda i, j: (0, 0)),
        ],
        out_specs=pl.BlockSpec((tm, d), lambda i, j: (i, 0)),
        out_shape=jax.ShapeDtypeStruct((t, d), F32),
        scratch_shapes=[pltpu.VMEM((tm, d), BF16)],
        compiler_params=_params(("parallel", "arbitrary")),
        name="mlp",
    )(x, g_pre.reshape(1, d), wu, wd, g_post.reshape(1, d))


def _mlstm_kernel(p_ref, gc_ref, gr_ref, bc_ref, br_ref, hg_ref, y_ref, c_ref, n_ref, m_ref):
    B, L = p_ref.shape[:2]
    H = ML_H
    prob = [(b, h) for b in range(B) for h in range(H)]

    @pl.when(pl.program_id(0) == 0)
    def _():
        c_ref[...] = jnp.zeros_like(c_ref)
        n_ref[...] = jnp.zeros_like(n_ref)
        m_ref[...] = jnp.zeros_like(m_ref)

    row = lax.broadcasted_iota(jnp.int32, (L, L), 0)
    col = lax.broadcasted_iota(jnp.int32, (L, L), 1)
    causal = row >= col
    tril = causal.astype(BF16)
    triu = (row <= col).astype(BF16)

    gc = GATE_SOFTCAP * jnp.tanh((gc_ref[...] + bc_ref[...]) / GATE_SOFTCAP)
    lf_c = -_softplus(-gc)
    bcum_c_all = [_cumsum_rows(tril, lf_c[b]) for b in range(B)]
    gr = GATE_SOFTCAP * jnp.tanh((gr_ref[...] + br_ref[...]) / GATE_SOFTCAP)
    lf_r = -_softplus(-gr)
    bcum_r_all = [_cumsum_cols(lf_r[b], triu) for b in range(B)]

    qo, ko, vo, oo = 0, H * ML_DQK, 2 * H * ML_DQK, 2 * H * ML_DQK + H * ML_DV
    q = jnp.stack([p_ref[b, :, qo + h * ML_DQK: qo + (h + 1) * ML_DQK] for b, h in prob]) * (ML_DQK ** -0.5)
    k = jnp.stack([p_ref[b, :, ko + h * ML_DQK: ko + (h + 1) * ML_DQK] for b, h in prob])
    v = jnp.stack([p_ref[b, :, vo + h * ML_DV: vo + (h + 1) * ML_DV] for b, h in prob])
    ig_c = jnp.stack([gc[b, :, h:h + 1] for b, h in prob])
    ig_r = jnp.stack([gr[b, h:h + 1, :] for b, h in prob])
    bcum_c = jnp.stack([bcum_c_all[b][:, H + h:H + h + 1] for b, h in prob])
    bcum_r = jnp.stack([bcum_r_all[b][H + h:H + h + 1, :] for b, h in prob])
    b_end = bcum_r[:, :, L - 1:L]
    m_prev = m_ref[...]
    c_prev = c_ref[...]
    n_prev = n_ref[...]

    dmat = jnp.where(causal, bcum_c - bcum_r + ig_r, NEG_BIG)
    inter = bcum_c + m_prev
    mt = jnp.maximum(inter, jnp.max(dmat, axis=-1, keepdims=True))
    a_inter = jnp.exp(inter - mt)
    pm = _bmm_nt(q, k) * jnp.exp(dmat - mt)
    num = a_inter * _bmm(q, c_prev.astype(BF16)) + _bmm(pm.astype(BF16), v)
    den = (a_inter * jnp.sum(q.astype(F32) * n_prev, axis=-1, keepdims=True)
           + jnp.sum(pm, axis=-1, keepdims=True))
    out = num * (1.0 / jnp.maximum(jnp.abs(den), jnp.exp(-mt)))

    m_new = jnp.maximum(b_end + m_prev, jnp.max(b_end - bcum_r + ig_r, axis=-1, keepdims=True))
    a_state = jnp.exp(b_end + m_prev - m_new)
    wk = k.astype(F32) * jnp.exp(b_end - bcum_c + ig_c - m_new)
    c_ref[...] = a_state * c_prev + _bmm_tn(wk.astype(BF16), v)
    n_ref[...] = a_state * n_prev + jnp.sum(wk, axis=1, keepdims=True)
    m_ref[...] = m_new

    for i, (b, h) in enumerate(prob):
        hs = _rms(out[i], hg_ref[:, h * ML_DV:(h + 1) * ML_DV])
        half = 0.5 * p_ref[b, :, oo + h * ML_DV: oo + (h + 1) * ML_DV].astype(F32)
        gate = 0.5 + 0.5 * jnp.tanh(half)
        y_ref[b, :, h * ML_DV:(h + 1) * ML_DV] = (hs * gate).astype(y_ref.dtype)


def mlstm_scan(p, gates_col, gates_row, bias_col, bias_row, head_g, batch, seq):
    L = min(ML_CHUNK, seq)
    nh = batch * ML_H
    y = pl.pallas_call(
        _mlstm_kernel,
        grid=(seq // L,),
        in_specs=[
            pl.BlockSpec((batch, L, p.shape[1]), lambda c: (0, c, 0)),
            pl.BlockSpec((batch, L, LANES), lambda c: (0, c, 0)),
            pl.BlockSpec((batch, 2 * ML_H, L), lambda c: (0, 0, c)),
            pl.BlockSpec((1, LANES), lambda c: (0, 0)),
            pl.BlockSpec((2 * ML_H, 1), lambda c: (0, 0)),
            pl.BlockSpec((1, ML_H * ML_DV), lambda c: (0, 0)),
        ],
        out_specs=pl.BlockSpec((batch, L, ML_H * ML_DV), lambda c: (0, c, 0)),
        out_shape=jax.ShapeDtypeStruct((batch, seq, ML_H * ML_DV), BF16),
        scratch_shapes=[
            pltpu.VMEM((nh, ML_DQK, ML_DV), F32),
            pltpu.VMEM((nh, 1, ML_DQK), F32),
            pltpu.VMEM((nh, 1, 1), F32),
        ],
        compiler_params=_params(("arbitrary",)),
        name="mlstm_scan",
    )(p.reshape(batch, seq, -1), gates_col.reshape(batch, seq, LANES), gates_row, bias_col, bias_row, head_g)
    return y.reshape(batch * seq, ML_H * ML_DV)


def _gdn_inproj_kernel(x_ref, g_ref, w_ref, wg_ref, cw_ref, o_ref, og_ref, h_ref, acc_ref, tail_ref, *,
                       blocks_per_batch, n_q, n_qk, n_qkv, dot_rows, conv_rows):
    i = pl.program_id(0)
    j = pl.program_id(1)
    tm, tn = o_ref.shape

    @pl.when(j == 0)
    def _():
        _rms_rows_to(h_ref, x_ref, g_ref)
        og_ref[...] = _dot(h_ref[...], wg_ref[...])

    def conv_tile(l2):
        first = (i % blocks_per_batch) == 0
        acc_ref[0:SUBLANES, :] = jnp.where(first, 0.0, tail_ref[j])
        cw = cw_ref[...]
        scale = jnp.where(j < n_q, GD ** -0.5, 1.0)
        for k0 in range(0, tm, dot_rows):
            acc_ref[SUBLANES + k0: SUBLANES + k0 + dot_rows, :] = _dot(h_ref[k0:k0 + dot_rows, :], w_ref[...])
            for r0 in range(k0, k0 + dot_rows, conv_rows):
                a = acc_ref[SUBLANES + r0: SUBLANES + r0 + conv_rows, :] * cw[CONV_K - 1:CONV_K, :]
                for s in range(1, CONV_K):
                    a = a + (acc_ref[SUBLANES - s + r0: SUBLANES - s + r0 + conv_rows, :]
                             * cw[CONV_K - 1 - s:CONV_K - s, :])
                half = 0.5 * a
                y = half + half * jnp.tanh(half)
                if l2:
                    for c in range(tn // GD):
                        blk = y[:, c * GD:(c + 1) * GD]
                        ss = jnp.sum(blk * blk, axis=-1, keepdims=True)
                        o_ref[r0:r0 + conv_rows, c * GD:(c + 1) * GD] = (
                            blk * (lax.rsqrt(ss + EPS) * scale)).astype(o_ref.dtype)
                else:
                    o_ref[r0:r0 + conv_rows, :] = y.astype(o_ref.dtype)
        tail_ref[j] = acc_ref[tm:tm + SUBLANES, :]

    @pl.when(j < n_qk)
    def _():
        conv_tile(True)

    @pl.when((j >= n_qk) & (j < n_qkv))
    def _():
        conv_tile(False)

    @pl.when(j >= n_qkv)
    def _():
        o_ref[...] = _dot(h_ref[...], w_ref[...]).astype(o_ref.dtype)


def gdn_inproj(x, g, w, layer, w_gate, conv_w, n, seq, tm, tn):
    t, d = x.shape
    n_qkv = conv_w.shape[1]
    tm = min(tm, seq)
    kern = functools.partial(
        _gdn_inproj_kernel, blocks_per_batch=seq // tm, n_q=GK_H * GD // tn, n_qk=2 * GK_H * GD // tn,
        n_qkv=n_qkv // tn, dot_rows=min(256, tm), conv_rows=64)
    last_conv_tile = n_qkv // tn - 1
    return pl.pallas_call(
        kern,
        grid=(t // tm, n // tn),
        in_specs=[
            pl.BlockSpec((tm, d), lambda i, j: (i, 0)),
            pl.BlockSpec((1, d), lambda i, j: (0, 0)),
            pl.BlockSpec((None, d, tn), lambda i, j: (layer, 0, j)),
            pl.BlockSpec((d, LANES), lambda i, j: (0, 0)),
            pl.BlockSpec((CONV_K, tn), lambda i, j: (0, jnp.minimum(j, last_conv_tile))),
        ],
        out_specs=[
            pl.BlockSpec((tm, tn), lambda i, j: (i, j)),
            pl.BlockSpec((tm, LANES), lambda i, j: (i, 0)),
        ],
        out_shape=[
            jax.ShapeDtypeStruct((t, n), BF16),
            jax.ShapeDtypeStruct((t, LANES), F32),
        ],
        scratch_shapes=[
            pltpu.VMEM((tm, d), BF16),
            pltpu.VMEM((tm + SUBLANES, tn), F32),
            pltpu.VMEM((n_qkv // tn, SUBLANES, tn), F32),
        ],
        compiler_params=_params(("arbitrary", "arbitrary")),
        name="gdn_inproj",
    )(x, g.reshape(1, d), w, w_gate, conv_w)


def _pair_blockdiag(y):
    L = y.shape[1]
    left = lax.broadcasted_iota(jnp.int32, (L, 2 * L), 1) < L
    zero = jnp.zeros_like(y)
    return jnp.concatenate([jnp.where(left, y, zero), jnp.where(left, zero, y)], axis=1)


def _pair_mm(x, y):
    return _bmm(x, _pair_blockdiag(y))


def _unit_lower_inverse_pairs(a, eye, same_block):
    L = a.shape[1]
    d = jnp.where(same_block[GDN_INV_BASE], a, 0.0)
    pw = (-d).astype(BF16)
    t = eye - d
    for _ in range(GDN_INV_BASE.bit_length() - 2):
        pw = _pair_mm(pw, pw).astype(BF16)
        t = t + _pair_mm(t.astype(BF16), pw)
    size = GDN_INV_BASE
    while size < L:
        off = jnp.where(same_block[2 * size] & jnp.logical_not(same_block[size]), a, 0.0)
        x = _pair_mm(off.astype(BF16), t.astype(BF16))
        t = t - _pair_mm(t.astype(BF16), x.astype(BF16))
        size *= 2
    return t


def _gdn_kernel(q_ref, k_ref, v_ref, z_ref, gc_ref, gr_ref, pc_ref, pr_ref, ng_ref, y_ref, s_ref):
    G = GDN_HEADS_PER_STEP
    L = GDN_CHUNK
    rows = q_ref.shape[0]
    nch = rows // L
    assert 2 * L == GD

    @pl.when(pl.program_id(2) == 0)
    def _():
        s_ref[...] = jnp.zeros_like(s_ref)

    row = lax.broadcasted_iota(jnp.int32, (L, 2 * L), 0)
    lane = lax.broadcasted_iota(jnp.int32, (L, 2 * L), 1)
    col = lane & (L - 1)
    left = lane < L
    causal = row >= col
    strict = row > col
    eye = (row == col).astype(F32)
    same_block = {}
    size = GDN_INV_BASE
    while size <= L:
        shift = size.bit_length() - 1
        same_block[size] = (row >> shift) == (col >> shift)
        size *= 2
    triu = (lax.broadcasted_iota(jnp.int32, (L, L), 0) <= lax.broadcasted_iota(jnp.int32, (L, L), 1)).astype(BF16)
    brow = lax.broadcasted_iota(jnp.int32, (rows, rows), 0)
    bcol = lax.broadcasted_iota(jnp.int32, (rows, rows), 1)
    lshift = L.bit_length() - 1
    chunk_tril = (((brow >> lshift) == (bcol >> lshift)) & (brow >= bcol)).astype(BF16)

    gcol = gc_ref[...]
    pc = pc_ref[...]
    beta_all = _sigmoid(gcol)
    g_c = -jnp.exp(pc[0:1, :]) * _softplus(gcol + pc[1:2, :])
    gam_c_all = _cumsum_rows(chunk_tril, g_c)
    pr = pr_ref[...]
    g_r = -jnp.exp(pr[:, 0:1]) * _softplus(gr_ref[...] + pr[:, 1:2])
    gam_r_all = _cumsum_cols(g_r.reshape(nch * 2 * G, L), triu).reshape(nch, 2 * G, L)

    order = [(cc, j) for cc in range(nch) for j in range(G)]
    pairs = [(cc, kh) for cc in range(nch) for kh in range(G // 2)]

    def rows_of(cc):
        return slice(cc * L, (cc + 1) * L)

    def lanes_of(h):
        return slice(h * GD, (h + 1) * GD)

    def to_pairs(x):
        return jnp.stack([jnp.where(left, x[2 * i], x[2 * i + 1]) for i in range(len(pairs))])

    def per_value_head(x):
        return jnp.stack([x[i // 2] for i in range(len(order))])

    def pad_rows(x):
        zero = jnp.zeros_like(x[0])
        return jnp.stack([jnp.concatenate([x[i], zero] if i % 2 == 0 else [zero, x[i]], axis=0)
                          for i in range(x.shape[0])])

    beta_b = jnp.stack([jnp.broadcast_to(beta_all[rows_of(cc), j:j + 1], (L, GD)) for cc, j in order])
    gam_b = jnp.stack([jnp.broadcast_to(gam_c_all[rows_of(cc), G + j:G + j + 1], (L, GD)) for cc, j in order])
    gam_last = jnp.stack([gam_r_all[cc, G + j:G + j + 1, L - 1:L] for cc, j in order])
    gam_r_p = jnp.stack([
        jnp.concatenate([gam_r_all[cc, G + 2 * kh:G + 2 * kh + 1, :],
                         gam_r_all[cc, G + 2 * kh + 1:G + 2 * kh + 2, :]], axis=-1)
        for cc, kh in pairs])

    k_k = jnp.stack([k_ref[rows_of(cc), lanes_of(kh)] for cc, kh in pairs])
    q_k = jnp.stack([q_ref[rows_of(cc), lanes_of(kh)] for cc, kh in pairs])
    k_twice = jnp.concatenate([k_k, k_k], axis=1)
    kk_p = _bmm_nt(k_k, k_twice)
    qk_p = _bmm_nt(q_k, k_twice)

    e = jnp.exp(jnp.where(causal, to_pairs(gam_b) - gam_r_p, NEG_BIG))
    a = to_pairs(beta_b) * kk_p * jnp.where(strict, e, 0.0)
    aqk = per_value_head((qk_p * e).astype(BF16))
    tinv = per_value_head(_unit_lower_inverse_pairs(a, eye, same_block).astype(BF16))

    kf = per_value_head(k_k).astype(F32)
    qf = per_value_head(q_k).astype(F32)
    v = jnp.stack([v_ref[rows_of(cc), lanes_of(j)] for cc, j in order]).astype(F32)
    egam_b = jnp.exp(gam_b)
    rhs = jnp.concatenate([v * beta_b, kf * beta_b * egam_b], axis=-1).astype(BF16)
    uw = _bmm(tinv, pad_rows(rhs))
    u = uw[:, :, :GD]
    lhs = jnp.concatenate([uw[:, :, GD:].astype(BF16), (qf * egam_b).astype(BF16)], axis=1)
    kd = (kf * jnp.exp(gam_last - gam_b)).astype(BF16)
    dl = jnp.exp(gam_last)

    s = s_ref[...]
    ng = ng_ref[...]
    for cc in range(nch):
        sl = slice(cc * G, (cc + 1) * G)
        ws_qs = _bmm(lhs[sl], s.astype(BF16))
        vn = (u[sl] - ws_qs[:, :L, :]).astype(BF16)
        o = ws_qs[:, L:, :] + _bmm(aqk[sl], pad_rows(vn))
        s = dl[sl] * s + _bmm_tn(kd[sl], vn)
        hs = _rms(o, ng)
        for j in range(G):
            half = 0.5 * z_ref[rows_of(cc), lanes_of(j)].astype(F32)
            silu_z = half + half * jnp.tanh(half)
            y_ref[rows_of(cc), lanes_of(j)] = (hs[j] * silu_z).astype(y_ref.dtype)
    s_ref[...] = s


def gdn_scan(p, z_col0, gates_col, gates_row, par_col, par_row, norm_g, batch, seq):
    t = p.shape[0]
    G = GDN_HEADS_PER_STEP
    L = GDN_CHUNK
    rows = min(GDN_ROWS_PER_STEP, seq)
    nrb = seq // rows
    nch = rows // L
    ng = GV_H // G
    qw = (G // 2) * GD
    vw = G * GD
    k_blk0 = GK_H * GD // qw
    v_blk0 = 2 * GK_H * GD // vw
    z_blk0 = z_col0 // vw
    return pl.pallas_call(
        _gdn_kernel,
        grid=(batch, ng, nrb),
        in_specs=[
            pl.BlockSpec((rows, qw), lambda b, g, c: (b * nrb + c, g)),
            pl.BlockSpec((rows, qw), lambda b, g, c: (b * nrb + c, k_blk0 + g)),
            pl.BlockSpec((rows, vw), lambda b, g, c: (b * nrb + c, v_blk0 + g)),
            pl.BlockSpec((rows, vw), lambda b, g, c: (b * nrb + c, z_blk0 + g)),
            pl.BlockSpec((None, rows, 2 * G), lambda b, g, c: (g, b * nrb + c, 0)),
            pl.BlockSpec((None, nch, 2 * G, L), lambda b, g, c: (g, b * nrb + c, 0, 0)),
            pl.BlockSpec((None, 2, 2 * G), lambda b, g, c: (g, 0, 0)),
            pl.BlockSpec((None, 2 * G, 2), lambda b, g, c: (g, 0, 0)),
            pl.BlockSpec((1, GD), lambda b, g, c: (0, 0)),
        ],
        out_specs=pl.BlockSpec((rows, vw), lambda b, g, c: (b * nrb + c, g)),
        out_shape=jax.ShapeDtypeStruct((t, GV_H * GD), BF16),
        scratch_shapes=[pltpu.VMEM((G, GD, GD), F32)],
        compiler_params=_params(("parallel", "parallel", "arbitrary")),
        name="gdn_scan",
    )(p, p, p, p, gates_col, gates_row, par_col, par_row, norm_g.reshape(1, GD))


def _gate_weights(w_in_layer, n_main):
    w = w_in_layer[:, n_main:]
    return jnp.pad(w, ((0, 0), (0, LANES - w.shape[1]))).astype(BF16)


def mlstm_layer(x, g_pre, g_post, w_in, layer, b_gates, head_g, w_out, batch, seq):
    n_main = 2 * ML_H * ML_DQK + 2 * ML_H * ML_DV
    p, gates_col = norm_matmul(x, g_pre, w_in.astype(BF16), layer, n_main,
                               _gate_weights(w_in[layer], n_main), 1024, 1024)
    gates_row = gates_col.reshape(batch, seq, LANES)[:, :, :2 * ML_H].transpose(0, 2, 1)
    bias_col = jnp.pad(b_gates.reshape(1, -1), ((0, 0), (0, LANES - 2 * ML_H)))
    bias_row = b_gates.reshape(-1, 1)
    y = mlstm_scan(p, gates_col, gates_row, bias_col, bias_row, head_g.reshape(1, -1), batch, seq)
    return matmul_norm_res(y, w_out.astype(BF16), layer, g_post, x, 512, 2048)


def gdn_layer(x, g_pre, g_post, w_in, layer, conv_w, a_log, dt_bias, norm_g, w_out, batch, seq):
    G = GDN_HEADS_PER_STEP
    ng = GV_H // G
    t = x.shape[0]
    n_qkv = 2 * GK_H * GD + GV_H * GD
    n_main = n_qkv + GV_H * GD
    p, gates = gdn_inproj(x, g_pre, w_in.astype(BF16), layer, _gate_weights(w_in[layer], n_main), conv_w,
                          n_main, seq, 1024, 1024)
    gates = gates[:, :2 * GV_H]
    gates_col = gates.reshape(t, 2, ng, G).transpose(2, 0, 1, 3).reshape(ng, t, 2 * G)
    gates_row = gates_col.reshape(ng, t // GDN_CHUNK, GDN_CHUNK, 2 * G).transpose(0, 1, 3, 2)
    zeros = jnp.zeros((ng, G), F32)
    a_grp = jnp.concatenate([zeros, a_log.reshape(ng, G)], axis=-1)
    dt_grp = jnp.concatenate([zeros, dt_bias.reshape(ng, G)], axis=-1)
    par_col = jnp.stack([a_grp, dt_grp], axis=1)
    par_row = par_col.transpose(0, 2, 1)
    y = gdn_scan(p, n_qkv, gates_col, gates_row, par_col, par_row, norm_g, batch, seq)
    return matmul_norm_res(y, w_out.astype(BF16), layer, g_post, x, 512, 2048)


def kernel(x, mem, norm_g, mem_norm_g, w_mem_kv, w_xq, w_xo, w_up, w_down, mlstm_w_in, mlstm_b_gates, mlstm_head_g, mlstm_w_out, gdn_w_in, gdn_conv_w, gdn_a_log, gdn_dt_bias, gdn_norm_g, gdn_w_out):
    batch, seq, d = x.shape
    depth = norm_g.shape[0]
    kv = mem_kv(mem.reshape(batch * MEM_LEN, d), mem_norm_g, w_mem_kv.astype(BF16))
    wq, wo = w_xq.astype(BF16), w_xo.astype(BF16)
    wu, wd = w_up.astype(BF16), w_down.astype(BF16)
    xt = x.reshape(batch * seq, d)
    for i in range(depth):
        g = norm_g[i]
        j = i // 2
        if i % 2 == 0:
            xt = mlstm_layer(xt, g[0], g[1], mlstm_w_in, j, mlstm_b_gates[j], mlstm_head_g[j],
                             mlstm_w_out, batch, seq)
        else:
            xt = gdn_layer(xt, g[0], g[1], gdn_w_in, j, gdn_conv_w[j], gdn_a_log[j], gdn_dt_bias[j],
                           gdn_norm_g[j], gdn_w_out, batch, seq)
        xt = xattn(xt, g[2], wq, kv, wo, i, g[3], seq, 512)
        xt = mlp(xt, g[4], wu, wd, i, g[5], 512, 1024)
    return xt.reshape(batch, seq, d)
```

```python
import functools

import jax
import jax.numpy as jnp
from jax import lax
from jax.experimental import pallas as pl
from jax.experimental.pallas import tpu as pltpu

EPS = 1e-6
GATE_SOFTCAP = 15.0
D_MODEL = 2048
ML_H = 4
ML_DV = D_MODEL // ML_H
ML_DQK = ML_DV // 2
ML_CHUNK = 256
GK_H = 16
GV_H = 32
GD = 128
CONV_K = 4
GDN_CHUNK = 64
GDN_HEADS_PER_STEP = 16
GDN_ROWS_PER_STEP = 256
GDN_INV_BASE = 8
MEM_LEN = 256
XA_H = 4
XA_DH = 128

LANES = 128
SUBLANES = 8
NEG_BIG = -1e30
VMEM_LIMIT = 56 * 1024 * 1024
RMS_ROWS = 16

BF16 = jnp.bfloat16
F32 = jnp.float32


def _params(sem):
    return pltpu.CompilerParams(dimension_semantics=sem, vmem_limit_bytes=VMEM_LIMIT)


def _rms(xf, g):
    ms = jnp.mean(xf * xf, axis=-1, keepdims=True)
    return xf * lax.rsqrt(ms + EPS) * g


def _rms_rows_to(dst_ref, src_ref, g_ref):
    g = g_ref[...]
    for r in range(0, src_ref.shape[0], RMS_ROWS):
        dst_ref[r:r + RMS_ROWS, :] = _rms(src_ref[r:r + RMS_ROWS, :], g).astype(dst_ref.dtype)


def _residual_rms_rows(o_ref, x_ref, g_ref):
    g = g_ref[...]
    for r in range(0, o_ref.shape[0], RMS_ROWS):
        o_ref[r:r + RMS_ROWS, :] = x_ref[r:r + RMS_ROWS, :] + _rms(o_ref[r:r + RMS_ROWS, :], g)


def _dot(a, b):
    return jnp.dot(a, b, preferred_element_type=F32)


def _dot_nt(a, b):
    return lax.dot_general(a, b, (((1,), (1,)), ((), ())), preferred_element_type=F32)


def _bmm(a, b):
    return jnp.einsum("pmk,pkn->pmn", a, b, preferred_element_type=F32)


def _bmm_nt(a, b):
    return jnp.einsum("pmk,pnk->pmn", a, b, preferred_element_type=F32)


def _bmm_tn(a, b):
    return jnp.einsum("pkm,pkn->pmn", a, b, preferred_element_type=F32)


def _split3(x):
    hi = x.astype(BF16)
    r = x - hi.astype(F32)
    mid = r.astype(BF16)
    lo = (r - mid.astype(F32)).astype(BF16)
    return hi, mid, lo


def _cumsum_rows(tri, x):
    hi, mid, lo = _split3(x)
    return _dot(tri, hi) + _dot(tri, mid) + _dot(tri, lo)


def _cumsum_cols(x, tri):
    hi, mid, lo = _split3(x)
    return _dot(hi, tri) + _dot(mid, tri) + _dot(lo, tri)


def _softplus(x):
    return jnp.maximum(x, 0.0) + jnp.log1p(jnp.exp(-jnp.abs(x)))


def _sigmoid(x):
    return 1.0 / (1.0 + jnp.exp(-x))


def _norm_matmul_kernel(x_ref, g_ref, w_ref, wg_ref, o_ref, og_ref, h_ref):
    j = pl.program_id(1)

    @pl.when(j == 0)
    def _():
        _rms_rows_to(h_ref, x_ref, g_ref)
        og_ref[...] = _dot(h_ref[...], wg_ref[...])
        o_ref[...] = _dot(h_ref[...], w_ref[...]).astype(o_ref.dtype)

    @pl.when(j > 0)
    def _():
        o_ref[...] = _dot(h_ref[...], w_ref[...]).astype(o_ref.dtype)


def norm_matmul(x, g, w, layer, n, w_gate, tm, tn):
    t, d = x.shape
    tm = min(tm, t)
    tn = min(tn, n)
    return pl.pallas_call(
        _norm_matmul_kernel,
        grid=(t // tm, n // tn),
        in_specs=[
            pl.BlockSpec((tm, d), lambda i, j: (i, 0)),
            pl.BlockSpec((1, d), lambda i, j: (0, 0)),
            pl.BlockSpec((None, d, tn), lambda i, j: (layer, 0, j)),
            pl.BlockSpec((d, LANES), lambda i, j: (0, 0)),
        ],
        out_specs=[
            pl.BlockSpec((tm, tn), lambda i, j: (i, j)),
            pl.BlockSpec((tm, LANES), lambda i, j: (i, 0)),
        ],
        out_shape=[
            jax.ShapeDtypeStruct((t, n), BF16),
            jax.ShapeDtypeStruct((t, LANES), F32),
        ],
        scratch_shapes=[pltpu.VMEM((tm, d), BF16)],
        compiler_params=_params(("parallel", "arbitrary")),
        name="norm_matmul",
    )(x, g.reshape(1, d), w, w_gate)


def _matmul_norm_res_kernel(y_ref, w_ref, g_ref, x_ref, o_ref, *, k_steps):
    if k_steps == 1:
        o_ref[...] = _dot(y_ref[...], w_ref[...])
        _residual_rms_rows(o_ref, x_ref, g_ref)
        return
    k = pl.program_id(1)

    @pl.when(k == 0)
    def _():
        o_ref[...] = _dot(y_ref[...], w_ref[...])

    @pl.when((k > 0) & (k < k_steps - 1))
    def _():
        o_ref[...] += _dot(y_ref[...], w_ref[...])

    @pl.when(k == k_steps - 1)
    def _():
        o_ref[...] += _dot(y_ref[...], w_ref[...])
        _residual_rms_rows(o_ref, x_ref, g_ref)


def matmul_norm_res(y, w, layer, g, x, tm, tk):
    t, kdim = y.shape
    d = w.shape[-1]
    tm = min(tm, t)
    tk = min(tk, kdim)
    return pl.pallas_call(
        functools.partial(_matmul_norm_res_kernel, k_steps=kdim // tk),
        grid=(t // tm, kdim // tk),
        in_specs=[
            pl.BlockSpec((tm, tk), lambda i, k: (i, k)),
            pl.BlockSpec((None, tk, d), lambda i, k: (layer, k, 0)),
            pl.BlockSpec((1, d), lambda i, k: (0, 0)),
            pl.BlockSpec((tm, d), lambda i, k: (i, 0)),
        ],
        out_specs=pl.BlockSpec((tm, d), lambda i, k: (i, 0)),
        out_shape=jax.ShapeDtypeStruct((t, d), F32),
        compiler_params=_params(("parallel", "arbitrary")),
        name="matmul_norm_res",
    )(y, w, g.reshape(1, d), x)


def _mem_kv_kernel(m_ref, g_ref, w_ref, o_ref):
    o_ref[...] = _dot(_rms(m_ref[...], g_ref[...]).astype(BF16), w_ref[...]).astype(o_ref.dtype)


def mem_kv(mem2d, g, w):
    t, d = mem2d.shape
    n = w.shape[1]
    tm = min(256, t)
    return pl.pallas_call(
        _mem_kv_kernel,
        grid=(t // tm,),
        in_specs=[
            pl.BlockSpec((tm, d), lambda i: (i, 0)),
            pl.BlockSpec((1, d), lambda i: (0, 0)),
            pl.BlockSpec((d, n), lambda i: (0, 0)),
        ],
        out_specs=pl.BlockSpec((tm, n), lambda i: (i, 0)),
        out_shape=jax.ShapeDtypeStruct((t, n), BF16),
        compiler_params=_params(("parallel",)),
        name="mem_kv",
    )(mem2d, g.reshape(1, d), w)


def _xattn_kernel(x_ref, g_pre_ref, wq_ref, kv_ref, wo_ref, g_post_ref, o_ref, h_ref):
    _rms_rows_to(h_ref, x_ref, g_pre_ref)
    q = _dot(h_ref[...], wq_ref[...]).astype(BF16)
    kv = kv_ref[...]
    qh = jnp.stack([q[:, a * XA_DH:(a + 1) * XA_DH] for a in range(XA_H)])
    kh = jnp.stack([kv[:, a * XA_DH:(a + 1) * XA_DH] for a in range(XA_H)])
    vh = jnp.stack([kv[:, (XA_H + a) * XA_DH:(XA_H + a + 1) * XA_DH] for a in range(XA_H)])
    s = _bmm_nt(qh, kh) * (XA_DH ** -0.5)
    s = s - jnp.max(s, axis=-1, keepdims=True)
    e = jnp.exp(s)
    p = e / jnp.sum(e, axis=-1, keepdims=True)
    oh = _bmm(p.astype(BF16), vh).astype(BF16)
    o = jnp.concatenate([oh[a] for a in range(XA_H)], axis=-1)
    o_ref[...] = _dot(o, wo_ref[...])
    _residual_rms_rows(o_ref, x_ref, g_post_ref)


def xattn(x, g_pre, wq, kv, wo, layer, g_post, seq, tm):
    t, d = x.shape
    tm = min(tm, seq)
    blocks_per_batch = seq // tm
    nq = wq.shape[-1]
    return pl.pallas_call(
        _xattn_kernel,
        grid=(t // tm,),
        in_specs=[
            pl.BlockSpec((tm, d), lambda i: (i, 0)),
            pl.BlockSpec((1, d), lambda i: (0, 0)),
            pl.BlockSpec((None, d, nq), lambda i: (layer, 0, 0)),
            pl.BlockSpec((MEM_LEN, 2 * nq), lambda i: (i // blocks_per_batch, 0)),
            pl.BlockSpec((None, nq, d), lambda i: (layer, 0, 0)),
            pl.BlockSpec((1, d), lambda i: (0, 0)),
        ],
        out_specs=pl.BlockSpec((tm, d), lambda i: (i, 0)),
        out_shape=jax.ShapeDtypeStruct((t, d), F32),
        scratch_shapes=[pltpu.VMEM((tm, d), BF16)],
        compiler_params=_params(("parallel",)),
        name="xattn",
    )(x, g_pre.reshape(1, d), wq, kv, wo, g_post.reshape(1, d))


def _mlp_kernel(x_ref, g_pre_ref, wu_ref, wd_ref, g_post_ref, o_ref, h_ref):
    j = pl.program_id(1)

    def partial_sum():
        u = jnp.maximum(_dot(h_ref[...], wu_ref[...]), 0.0)
        return _dot((u * u).astype(BF16), wd_ref[...])

    @pl.when(j == 0)
    def _():
        _rms_rows_to(h_ref, x_ref, g_pre_ref)
        o_ref[...] = partial_sum()

    last = pl.num_programs(1) - 1

    @pl.when((j > 0) & (j < last))
    def _():
        o_ref[...] += partial_sum()

    @pl.when(j == last)
    def _():
        o_ref[...] += partial_sum()
        _residual_rms_rows(o_ref, x_ref, g_post_ref)


def mlp(x, g_pre, wu, wd, layer, g_post, tm, tf):
    t, d = x.shape
    f = wu.shape[-1]
    tm = min(tm, t)
    assert f // tf >= 2
    return pl.pallas_call(
        _mlp_kernel,
        grid=(t // tm, f // tf),
        in_specs=[
            pl.BlockSpec((tm, d), lambda i, j: (i, 0)),
            pl.BlockSpec((1, d), lambda i, j: (0, 0)),
            pl.BlockSpec((None, d, tf), lambda i, j: (layer, 0, j)),
            pl.BlockSpec((None, tf, d), lambda i, j: (layer, j, 0)),
            pl.BlockSpec((1, d), lambda i, j: (0, 0)),
        ],
        out_specs=pl.BlockSpec((tm, d), lambda i, j: (i, 0)),
        out_shape=jax.ShapeDtypeStruct((t, d), F32),
        scratch_shapes=[pltpu.VMEM((tm, d), BF16)],
        compiler_params=_params(("parallel", "arbitrary")),
        name="mlp",
    )(x, g_pre.reshape(1, d), wu, wd, g_post.reshape(1, d))


def _mlstm_kernel(p_ref, gc_ref, gr_ref, bc_ref, br_ref, hg_ref, y_ref, c_ref, n_ref, m_ref):
    B, L = p_ref.shape[:2]
    H = ML_H
    prob = [(b, h) for b in range(B) for h in range(H)]

    @pl.when(pl.program_id(0) == 0)
    def _():
        c_ref[...] = jnp.zeros_like(c_ref)
        n_ref[...] = jnp.zeros_like(n_ref)
        m_ref[...] = jnp.zeros_like(m_ref)

    row = lax.broadcasted_iota(jnp.int32, (L, L), 0)
    col = lax.broadcasted_iota(jnp.int32, (L, L), 1)
    causal = row >= col
    tril = causal.astype(BF16)
    triu = (row <= col).astype(BF16)

    gc = GATE_SOFTCAP * jnp.tanh((gc_ref[...] + bc_ref[...]) / GATE_SOFTCAP)
    lf_c = -_softplus(-gc)
    bcum_c_all = [_cumsum_rows(tril, lf_c[b]) for b in range(B)]
    gr = GATE_SOFTCAP * jnp.tanh((gr_ref[...] + br_ref[...]) / GATE_SOFTCAP)
    lf_r = -_softplus(-gr)
    bcum_r_all = [_cumsum_cols(lf_r[b], triu) for b in range(B)]

    qo, ko, vo, oo = 0, H * ML_DQK, 2 * H * ML_DQK, 2 * H * ML_DQK + H * ML_DV
    q = jnp.stack([p_ref[b, :, qo + h * ML_DQK: qo + (h + 1) * ML_DQK] for b, h in prob]) * (ML_DQK ** -0.5)
    k = jnp.stack([p_ref[b, :, ko + h * ML_DQK: ko + (h + 1) * ML_DQK] for b, h in prob])
    v = jnp.stack([p_ref[b, :, vo + h * ML_DV: vo + (h + 1) * ML_DV] for b, h in prob])
    ig_c = jnp.stack([gc[b, :, h:h + 1] for b, h in prob])
    ig_r = jnp.stack([gr[b, h:h + 1, :] for b, h in prob])
    bcum_c = jnp.stack([bcum_c_all[b][:, H + h:H + h + 1] for b, h in prob])
    bcum_r = jnp.stack([bcum_r_all[b][H + h:H + h + 1, :] for b, h in prob])
    b_end = bcum_r[:, :, L - 1:L]
    m_prev = m_ref[...]
    c_prev = c_ref[...]
    n_prev = n_ref[...]

    dmat = jnp.where(causal, bcum_c - bcum_r + ig_r, NEG_BIG)
    inter = bcum_c + m_prev
    mt = jnp.maximum(inter, jnp.max(dmat, axis=-1, keepdims=True))
    a_inter = jnp.exp(inter - mt)
    pm = _bmm_nt(q, k) * jnp.exp(dmat - mt)
    num = a_inter * _bmm(q, c_prev.astype(BF16)) + _bmm(pm.astype(BF16), v)
    den = (a_inter * jnp.sum(q.astype(F32) * n_prev, axis=-1, keepdims=True)
           + jnp.sum(pm, axis=-1, keepdims=True))
    out = num * (1.0 / jnp.maximum(jnp.abs(den), jnp.exp(-mt)))

    m_new = jnp.maximum(b_end + m_prev, jnp.max(b_end - bcum_r + ig_r, axis=-1, keepdims=True))
    a_state = jnp.exp(b_end + m_prev - m_new)
    wk = k.astype(F32) * jnp.exp(b_end - bcum_c + ig_c - m_new)
    c_ref[...] = a_state * c_prev + _bmm_tn(wk.astype(BF16), v)
    n_ref[...] = a_state * n_prev + jnp.sum(wk, axis=1, keepdims=True)
    m_ref[...] = m_new

    for i, (b, h) in enumerate(prob):
        hs = _rms(out[i], hg_ref[:, h * ML_DV:(h + 1) * ML_DV])
        half = 0.5 * p_ref[b, :, oo + h * ML_DV: oo + (h + 1) * ML_DV].astype(F32)
        gate = 0.5 + 0.5 * jnp.tanh(half)
        y_ref[b, :, h * ML_DV:(h + 1) * ML_DV] = (hs * gate).astype(y_ref.dtype)


def mlstm_scan(p, gates_col, gates_row, bias_col, bias_row, head_g, batch, seq):
    L = min(ML_CHUNK, seq)
    nh = batch * ML_H
    y = pl.pallas_call(
        _mlstm_kernel,
        grid=(seq // L,),
        in_specs=[
            pl.BlockSpec((batch, L, p.shape[1]), lambda c: (0, c, 0)),
            pl.BlockSpec((batch, L, LANES), lambda c: (0, c, 0)),
            pl.BlockSpec((batch, 2 * ML_H, L), lambda c: (0, 0, c)),
            pl.BlockSpec((1, LANES), lambda c: (0, 0)),
            pl.BlockSpec((2 * ML_H, 1), lambda c: (0, 0)),
            pl.BlockSpec((1, ML_H * ML_DV), lambda c: (0, 0)),
        ],
        out_specs=pl.BlockSpec((batch, L, ML_H * ML_DV), lambda c: (0, c, 0)),
        out_shape=jax.ShapeDtypeStruct((batch, seq, ML_H * ML_DV), BF16),
        scratch_shapes=[
            pltpu.VMEM((nh, ML_DQK, ML_DV), F32),
            pltpu.VMEM((nh, 1, ML_DQK), F32),
            pltpu.VMEM((nh, 1, 1), F32),
        ],
        compiler_params=_params(("arbitrary",)),
        name="mlstm_scan",
    )(p.reshape(batch, seq, -1), gates_col.reshape(batch, seq, LANES), gates_row, bias_col, bias_row, head_g)
    return y.reshape(batch * seq, ML_H * ML_DV)


def _gdn_inproj_kernel(x_ref, g_ref, w_ref, wg_ref, cw_ref, o_ref, og_ref, h_ref, acc_ref, tail_ref, *,
                       blocks_per_batch, n_q, n_qk, n_qkv, dot_rows, conv_rows):
    i = pl.program_id(0)
    j = pl.program_id(1)
    tm, tn = o_ref.shape

    def conv_tile(l2):
        first = (i % blocks_per_batch) == 0
        acc_ref[0:SUBLANES, :] = jnp.where(first, 0.0, tail_ref[j])
        cw = cw_ref[...]
        scale = jnp.where(j < n_q, GD ** -0.5, 1.0)
        for k0 in range(0, tm, dot_rows):
            acc_ref[SUBLANES + k0: SUBLANES + k0 + dot_rows, :] = _dot(h_ref[k0:k0 + dot_rows, :], w_ref[...])
            for r0 in range(k0, k0 + dot_rows, conv_rows):
                a = acc_ref[SUBLANES + r0: SUBLANES + r0 + conv_rows, :] * cw[CONV_K - 1:CONV_K, :]
                for s in range(1, CONV_K):
                    a = a + (acc_ref[SUBLANES - s + r0: SUBLANES - s + r0 + conv_rows, :]
                             * cw[CONV_K - 1 - s:CONV_K - s, :])
                half = 0.5 * a
                y = half + half * jnp.tanh(half)
                if l2:
                    for c in range(tn // GD):
                        blk = y[:, c * GD:(c + 1) * GD]
                        ss = jnp.sum(blk * blk, axis=-1, keepdims=True)
                        o_ref[r0:r0 + conv_rows, c * GD:(c + 1) * GD] = (
                            blk * (lax.rsqrt(ss + EPS) * scale)).astype(o_ref.dtype)
                else:
                    o_ref[r0:r0 + conv_rows, :] = y.astype(o_ref.dtype)
        tail_ref[j] = acc_ref[tm:tm + SUBLANES, :]

    @pl.when(j == 0)
    def _():
        _rms_rows_to(h_ref, x_ref, g_ref)
        og_ref[...] = _dot(h_ref[...], wg_ref[...])
        conv_tile(True)

    @pl.when((j > 0) & (j < n_qk))
    def _():
        conv_tile(True)

    @pl.when((j >= n_qk) & (j < n_qkv))
    def _():
        conv_tile(False)

    @pl.when(j >= n_qkv)
    def _():
        o_ref[...] = _dot(h_ref[...], w_ref[...]).astype(o_ref.dtype)


def gdn_inproj(x, g, w, layer, w_gate, conv_w, n, seq, tm, tn):
    t, d = x.shape
    n_qkv = conv_w.shape[1]
    tm = min(tm, seq)
    kern = functools.partial(
        _gdn_inproj_kernel, blocks_per_batch=seq // tm, n_q=GK_H * GD // tn, n_qk=2 * GK_H * GD // tn,
        n_qkv=n_qkv // tn, dot_rows=min(256, tm), conv_rows=64)
    last_conv_tile = n_qkv // tn - 1
    return pl.pallas_call(
        kern,
        grid=(t // tm, n // tn),
        in_specs=[
            pl.BlockSpec((tm, d), lambda i, j: (i, 0)),
            pl.BlockSpec((1, d), lambda i, j: (0, 0)),
            pl.BlockSpec((None, d, tn), lambda i, j: (layer, 0, j)),
            pl.BlockSpec((d, LANES), lambda i, j: (0, 0)),
            pl.BlockSpec((CONV_K, tn), lambda i, j: (0, jnp.minimum(j, last_conv_tile))),
        ],
        out_specs=[
            pl.BlockSpec((tm, tn), lambda i, j: (i, j)),
            pl.BlockSpec((tm, LANES), lambda i, j: (i, 0)),
        ],
        out_shape=[
            jax.ShapeDtypeStruct((t, n), BF16),
            jax.ShapeDtypeStruct((t, LANES), F32),
        ],
        scratch_shapes=[
            pltpu.VMEM((tm, d), BF16),
            pltpu.VMEM((tm + SUBLANES, tn), F32),
            pltpu.VMEM((n_qkv // tn, SUBLANES, tn), F32),
        ],
        compiler_params=_params(("arbitrary", "arbitrary")),
        name="gdn_inproj",
    )(x, g.reshape(1, d), w, w_gate, conv_w)


def _pair_blockdiag(y):
    L = y.shape[1]
    left = lax.broadcasted_iota(jnp.int32, (L, 2 * L), 1) < L
    zero = jnp.zeros_like(y)
    return jnp.concatenate([jnp.where(left, y, zero), jnp.where(left, zero, y)], axis=1)


def _pair_mm(x, y):
    return _bmm(x, _pair_blockdiag(y))


def _unit_lower_inverse_pairs(a, eye, same_block):
    L = a.shape[1]
    d = jnp.where(same_block[GDN_INV_BASE], a, 0.0)
    pw = (-d).astype(BF16)
    t = eye - d
    for _ in range(GDN_INV_BASE.bit_length() - 2):
        pw = _pair_mm(pw, pw).astype(BF16)
        t = t + _pair_mm(t.astype(BF16), pw)
    size = GDN_INV_BASE
    while size < L:
        off = jnp.where(same_block[2 * size] & jnp.logical_not(same_block[size]), a, 0.0)
        x = _pair_mm(off.astype(BF16), t.astype(BF16))
        t = t - _pair_mm(t.astype(BF16), x.astype(BF16))
        size *= 2
    return t


def _gdn_kernel(q_ref, k_ref, v_ref, z_ref, gc_ref, gr_ref, pc_ref, pr_ref, ng_ref, y_ref, s_ref):
    G = GDN_HEADS_PER_STEP
    L = GDN_CHUNK
    rows = q_ref.shape[0]
    nch = rows // L
    assert 2 * L == GD

    @pl.when(pl.program_id(2) == 0)
    def _():
        s_ref[...] = jnp.zeros_like(s_ref)

    row = lax.broadcasted_iota(jnp.int32, (L, 2 * L), 0)
    lane = lax.broadcasted_iota(jnp.int32, (L, 2 * L), 1)
    col = lane & (L - 1)
    left = lane < L
    causal = row >= col
    strict = row > col
    eye = (row == col).astype(F32)
    same_block = {}
    size = GDN_INV_BASE
    while size <= L:
        shift = size.bit_length() - 1
        same_block[size] = (row >> shift) == (col >> shift)
        size *= 2
    triu = (lax.broadcasted_iota(jnp.int32, (L, L), 0) <= lax.broadcasted_iota(jnp.int32, (L, L), 1)).astype(BF16)
    brow = lax.broadcasted_iota(jnp.int32, (rows, rows), 0)
    bcol = lax.broadcasted_iota(jnp.int32, (rows, rows), 1)
    lshift = L.bit_length() - 1
    chunk_tril = (((brow >> lshift) == (bcol >> lshift)) & (brow >= bcol)).astype(BF16)

    gcol = gc_ref[...]
    pc = pc_ref[...]
    beta_all = _sigmoid(gcol)
    g_c = -jnp.exp(pc[0:1, :]) * _softplus(gcol + pc[1:2, :])
    gam_c_all = _cumsum_rows(chunk_tril, g_c)
    pr = pr_ref[...]
    g_r = -jnp.exp(pr[:, 0:1]) * _softplus(gr_ref[...] + pr[:, 1:2])
    gam_r_all = _cumsum_cols(g_r.reshape(nch * 2 * G, L), triu).reshape(nch, 2 * G, L)

    order = [(cc, j) for cc in range(nch) for j in range(G)]
    pairs = [(cc, kh) for cc in range(nch) for kh in range(G // 2)]

    def rows_of(cc):
        return slice(cc * L, (cc + 1) * L)

    def lanes_of(h):
        return slice(h * GD, (h + 1) * GD)

    def to_pairs(x):
        return jnp.stack([jnp.where(left, x[2 * i], x[2 * i + 1]) for i in range(len(pairs))])

    def per_value_head(x):
        return jnp.stack([x[i // 2] for i in range(len(order))])

    def pad_rows(x):
        zero = jnp.zeros_like(x[0])
        return jnp.stack([jnp.concatenate([x[i], zero] if i % 2 == 0 else [zero, x[i]], axis=0)
                          for i in range(x.shape[0])])

    beta_b = jnp.stack([jnp.broadcast_to(beta_all[rows_of(cc), j:j + 1], (L, GD)) for cc, j in order])
    gam_b = jnp.stack([jnp.broadcast_to(gam_c_all[rows_of(cc), G + j:G + j + 1], (L, GD)) for cc, j in order])
    gam_last = jnp.stack([gam_r_all[cc, G + j:G + j + 1, L - 1:L] for cc, j in order])
    gam_r_p = jnp.stack([
        jnp.concatenate([gam_r_all[cc, G + 2 * kh:G + 2 * kh + 1, :],
                         gam_r_all[cc, G + 2 * kh + 1:G + 2 * kh + 2, :]], axis=-1)
        for cc, kh in pairs])

    k_k = jnp.stack([k_ref[rows_of(cc), lanes_of(kh)] for cc, kh in pairs])
    q_k = jnp.stack([q_ref[rows_of(cc), lanes_of(kh)] for cc, kh in pairs])
    k_twice = jnp.concatenate([k_k, k_k], axis=1)
    kk_p = _bmm_nt(k_k, k_twice)
    qk_p = _bmm_nt(q_k, k_twice)

    e = jnp.exp(jnp.where(causal, to_pairs(gam_b) - gam_r_p, NEG_BIG))
    a = to_pairs(beta_b) * kk_p * jnp.where(strict, e, 0.0)
    aqk = per_value_head((qk_p * e).astype(BF16))
    tinv = per_value_head(_unit_lower_inverse_pairs(a, eye, same_block).astype(BF16))

    kf = per_value_head(k_k).astype(F32)
    qf = per_value_head(q_k).astype(F32)
    v = jnp.stack([v_ref[rows_of(cc), lanes_of(j)] for cc, j in order]).astype(F32)
    egam_b = jnp.exp(gam_b)
    rhs = jnp.concatenate([v * beta_b, kf * beta_b * egam_b], axis=-1).astype(BF16)
    uw = _bmm(tinv, pad_rows(rhs))
    u = uw[:, :, :GD]
    lhs = jnp.concatenate([uw[:, :, GD:].astype(BF16), (qf * egam_b).astype(BF16)], axis=1)
    kd = (kf * jnp.exp(gam_last - gam_b)).astype(BF16)
    dl = jnp.exp(gam_last)

    s = s_ref[...]
    ng = ng_ref[...]
    for cc in range(nch):
        sl = slice(cc * G, (cc + 1) * G)
        ws_qs = _bmm(lhs[sl], s.astype(BF16))
        vn = (u[sl] - ws_qs[:, :L, :]).astype(BF16)
        o = ws_qs[:, L:, :] + _bmm(aqk[sl], pad_rows(vn))
        s = dl[sl] * s + _bmm_tn(kd[sl], vn)
        hs = _rms(o, ng)
        for j in range(G):
            half = 0.5 * z_ref[rows_of(cc), lanes_of(j)].astype(F32)
            silu_z = half + half * jnp.tanh(half)
            y_ref[rows_of(cc), lanes_of(j)] = (hs[j] * silu_z).astype(y_ref.dtype)
    s_ref[...] = s


def gdn_scan(p, z_col0, gates_col, gates_row, par_col, par_row, norm_g, batch, seq):
    t = p.shape[0]
    G = GDN_HEADS_PER_STEP
    L = GDN_CHUNK
    rows = min(GDN_ROWS_PER_STEP, seq)
    nrb = seq // rows
    nch = rows // L
    ng = GV_H // G
    qw = (G // 2) * GD
    vw = G * GD
    k_blk0 = GK_H * GD // qw
    v_blk0 = 2 * GK_H * GD // vw
    z_blk0 = z_col0 // vw
    return pl.pallas_call(
        _gdn_kernel,
        grid=(batch, ng, nrb),
        in_specs=[
            pl.BlockSpec((rows, qw), lambda b, g, c: (b * nrb + c, g)),
            pl.BlockSpec((rows, qw), lambda b, g, c: (b * nrb + c, k_blk0 + g)),
            pl.BlockSpec((rows, vw), lambda b, g, c: (b * nrb + c, v_blk0 + g)),
            pl.BlockSpec((rows, vw), lambda b, g, c: (b * nrb + c, z_blk0 + g)),
            pl.BlockSpec((None, rows, 2 * G), lambda b, g, c: (g, b * nrb + c, 0)),
            pl.BlockSpec((None, nch, 2 * G, L), lambda b, g, c: (g, b * nrb + c, 0, 0)),
            pl.BlockSpec((None, 2, 2 * G), lambda b, g, c: (g, 0, 0)),
            pl.BlockSpec((None, 2 * G, 2), lambda b, g, c: (g, 0, 0)),
            pl.BlockSpec((1, GD), lambda b, g, c: (0, 0)),
        ],
        out_specs=pl.BlockSpec((rows, vw), lambda b, g, c: (b * nrb + c, g)),
        out_shape=jax.ShapeDtypeStruct((t, GV_H * GD), BF16),
        scratch_shapes=[pltpu.VMEM((G, GD, GD), F32)],
        compiler_params=_params(("parallel", "parallel", "arbitrary")),
        name="gdn_scan",
    )(p, p, p, p, gates_col, gates_row, par_col, par_row, norm_g.reshape(1, GD))


def _gate_weights(w_in_layer, n_main):
    w = w_in_layer[:, n_main:]
    return jnp.pad(w, ((0, 0), (0, LANES - w.shape[1]))).astype(BF16)


def mlstm_layer(x, g_pre, g_post, w_in, layer, b_gates, head_g, w_out, batch, seq):
    n_main = 2 * ML_H * ML_DQK + 2 * ML_H * ML_DV
    p, gates_col = norm_matmul(x, g_pre, w_in.astype(BF16), layer, n_main,
                               _gate_weights(w_in[layer], n_main), 1024, 1024)
    gates_row = gates_col.reshape(batch, seq, LANES)[:, :, :2 * ML_H].transpose(0, 2, 1)
    bias_col = jnp.pad(b_gates.reshape(1, -1), ((0, 0), (0, LANES - 2 * ML_H)))
    bias_row = b_gates.reshape(-1, 1)
    y = mlstm_scan(p, gates_col, gates_row, bias_col, bias_row, head_g.reshape(1, -1), batch, seq)
    return matmul_norm_res(y, w_out.astype(BF16), layer, g_post, x, 512, 2048)


def gdn_layer(x, g_pre, g_post, w_in, layer, conv_w, a_log, dt_bias, norm_g, w_out, batch, seq):
    G = GDN_HEADS_PER_STEP
    ng = GV_H // G
    t = x.shape[0]
    n_qkv = 2 * GK_H * GD + GV_H * GD
    n_main = n_qkv + GV_H * GD
    p, gates = gdn_inproj(x, g_pre, w_in.astype(BF16), layer, _gate_weights(w_in[layer], n_main), conv_w,
                          n_main, seq, 1024, 1024)
    gates = gates[:, :2 * GV_H]
    gates_col = gates.reshape(t, 2, ng, G).transpose(2, 0, 1, 3).reshape(ng, t, 2 * G)
    gates_row = gates_col.reshape(ng, t // GDN_CHUNK, GDN_CHUNK, 2 * G).transpose(0, 1, 3, 2)
    zeros = jnp.zeros((ng, G), F32)
    a_grp = jnp.concatenate([zeros, a_log.reshape(ng, G)], axis=-1)
    dt_grp = jnp.concatenate([zeros, dt_bias.reshape(ng, G)], axis=-1)
    par_col = jnp.stack([a_grp, dt_grp], axis=1)
    par_row = par_col.transpose(0, 2, 1)
    y = gdn_scan(p, n_qkv, gates_col, gates_row, par_col, par_row, norm_g, batch, seq)
    return matmul_norm_res(y, w_out.astype(BF16), layer, g_post, x, 512, 2048)


def kernel(x, mem, norm_g, mem_norm_g, w_mem_kv, w_xq, w_xo, w_up, w_down, mlstm_w_in, mlstm_b_gates, mlstm_head_g, mlstm_w_out, gdn_w_in, gdn_conv_w, gdn_a_log, gdn_dt_bias, gdn_norm_g, gdn_w_out):
    batch, seq, d = x.shape
    depth = norm_g.shape[0]
    kv = mem_kv(mem.reshape(batch * MEM_LEN, d), mem_norm_g, w_mem_kv.astype(BF16))
    wq, wo = w_xq.astype(BF16), w_xo.astype(BF16)
    wu, wd = w_up.astype(BF16), w_down.astype(BF16)
    xt = x.reshape(batch * seq, d)
    for i in range(depth):
        g = norm_g[i]
        j = i // 2
        if i % 2 == 0:
            xt = mlstm_layer(xt, g[0], g[1], mlstm_w_in, j, mlstm_b_gates[j], mlstm_head_g[j],
                             mlstm_w_out, batch, seq)
        else:
            xt = gdn_layer(xt, g[0], g[1], gdn_w_in, j, gdn_conv_w[j], gdn_a_log[j], gdn_dt_bias[j],
                           gdn_norm_g[j], gdn_w_out, batch, seq)
        xt = xattn(xt, g[2], wq, kv, wo, i, g[3], seq, 512)
        xt = mlp(xt, g[4], wu, wd, i, g[5], 512, 1024)
    return xt.reshape(batch, seq, d)
```
